```python
import math
import jax, jax.numpy as jnp
from jax import lax
import numpy as np

D_MODEL = 1024
BATCH = 4
SEQ = 4096
DEPTH = 2

N_A_LAYERS = DEPTH // 2
N_B_LAYERS = DEPTH - N_A_LAYERS

RET_HEADS = 4
RET_QK_DIM = 256
RET_V_DIM = 512
RET_WIDTH = RET_HEADS * RET_V_DIM
RET_QK_WIDTH = RET_HEADS * RET_QK_DIM
RET_CHUNK = 128
ROPE_BASE = 10000.0

SB_HEADS = 8
SB_QK_DIM = 128
SB_V_DIM = 256
SB_QK_WIDTH = SB_HEADS * SB_QK_DIM
SB_WIDTH = SB_HEADS * SB_V_DIM
SB_BLOCK = 128

EPS = 1e-6

kernel_name = "yoco_retention_stickbreaking_hybrid"


def rms_norm(x, g):
    xf = x.astype(jnp.float32)
    y = xf * lax.rsqrt(jnp.mean(xf * xf, axis=-1, keepdims=True) + EPS)
    return (y * g.astype(jnp.float32)).astype(x.dtype)


def split_heads(t, n_heads):
    b, s, w = t.shape
    return t.reshape(b, s, n_heads, w // n_heads).transpose(0, 2, 1, 3)


def merge_heads(t):
    b, h, s, d = t.shape
    return t.transpose(0, 2, 1, 3).reshape(b, s, h * d)


def rope(x, pos):
    half = x.shape[-1] // 2
    inv_freq = 1.0 / (ROPE_BASE ** (jnp.arange(half, dtype=jnp.float32) / half))
    ang = pos[:, None] * inv_freq[None, :]
    cos, sin = jnp.cos(ang), jnp.sin(ang)
    x1, x2 = x[..., :half], x[..., half:]
    return jnp.concatenate([x1 * cos - x2 * sin, x1 * sin + x2 * cos], axis=-1)


def retention_chunkwise(q, k, v):
    b, h, t, dk = q.shape
    dv = v.shape[-1]
    c = RET_CHUNK
    nc = t // c
    log_gamma = jnp.log1p(-jnp.exp2(-5.0 - jnp.arange(h, dtype=jnp.float32)))
    idx = jnp.arange(c, dtype=jnp.float32)
    diff = idx[:, None] - idx[None, :]
    intra_decay = jnp.where(diff >= 0, jnp.exp(log_gamma[:, None, None] * diff), 0.0)
    q_decay = jnp.exp(log_gamma[:, None] * (idx[None, :] + 1.0))
    k_decay = jnp.exp(log_gamma[:, None] * (c - 1.0 - idx[None, :]))
    chunk_decay = jnp.exp(log_gamma * c)

    def to_chunks(a):
        return a.reshape(b, h, nc, c, a.shape[-1]).transpose(2, 0, 1, 3, 4)

    def step(state, inp):
        qc, kc, vc = inp
        scores = jnp.einsum('bhid,bhjd->bhij', qc, kc) * intra_decay
        o = jnp.einsum('bhij,bhje->bhie', scores, vc)
        o = o + jnp.einsum('bhid,bhde->bhie', qc * q_decay[None, :, :, None], state)
        state = state * chunk_decay[None, :, None, None] + jnp.einsum(
            'bhjd,bhje->bhde', kc * k_decay[None, :, :, None], vc)
        return state, o

    state0 = jnp.zeros((b, h, dk, dv), jnp.float32)
    _, out = lax.scan(step, state0, (to_chunks(q), to_chunks(k), to_chunks(v)))
    return out.transpose(1, 2, 0, 3, 4).reshape(b, h, t, dv)


def stick_breaking_attention(q, k, v):
    b, h, t, dk = q.shape
    nb = t // SB_BLOCK
    scale = 1.0 / math.sqrt(dk)
    key_idx = jnp.arange(t)
    q_blocks = q.reshape(b, h, nb, SB_BLOCK, dk).transpose(2, 0, 1, 3, 4)

    def block(args):
        qb, blk = args
        t_idx = blk * SB_BLOCK + jnp.arange(SB_BLOCK)
        z = jnp.einsum('bhqd,bhkd->bhqk', qb, k) * scale
        causal = key_idx[None, :] < t_idx[:, None]
        log_not = jnp.where(causal, jax.nn.log_sigmoid(-z), 0.0)
        suffix = lax.cumsum(log_not, axis=3, reverse=True) - log_not
        weights = jnp.where(causal, jnp.exp(jax.nn.log_sigmoid(z) + suffix), 0.0)
        return jnp.einsum('bhqk,bhke->bhqe', weights, v)

    out = lax.map(block, (q_blocks, jnp.arange(nb)))
    return out.transpose(1, 2, 0, 3, 4).reshape(b, h, t, v.shape[-1])


def setup_inputs(seed: int = 0) -> dict:
    key = jax.random.key(seed)
    ks = jax.random.split(key, 12)
    d = D_MODEL
    ret_in_cols = 2 * RET_QK_WIDTH + 2 * RET_WIDTH
    sb_in_cols = SB_QK_WIDTH + SB_WIDTH
    kv_cols = SB_QK_WIDTH + SB_WIDTH

    def gain(k, shape):
        return 1.0 + 0.02 * jax.random.normal(k, shape, jnp.float32)

    def dense(k, shape, fan_in):
        return jax.random.normal(k, shape, jnp.float32) * (fan_in ** -0.5)

    return {
        "x": jax.random.normal(ks[0], (BATCH, SEQ, d), jnp.float32),
        "ret_norm_pre": gain(ks[1], (N_A_LAYERS, d)),
        "ret_w_in": dense(ks[2], (N_A_LAYERS, d, ret_in_cols), d),
        "ret_w_out": dense(ks[3], (N_A_LAYERS, RET_WIDTH, d), RET_WIDTH),
        "ret_norm_post": gain(ks[4], (N_A_LAYERS, d)),
        "kv_norm": gain(ks[5], (d,)),
        "w_kv": dense(ks[6], (d, kv_cols), d),
        "sb_norm_pre": gain(ks[7], (N_B_LAYERS, d)),
        "sb_w_in": dense(ks[8], (N_B_LAYERS, d, sb_in_cols), d),
        "sb_w_out": dense(ks[9], (N_B_LAYERS, SB_WIDTH, d), SB_WIDTH),
        "sb_norm_post": gain(ks[10], (N_B_LAYERS, d)),
    }


def reference(x, ret_norm_pre, ret_w_in, ret_w_out, ret_norm_post, kv_norm, w_kv,
              sb_norm_pre, sb_w_in, sb_w_out, sb_norm_post):
    b, t, _ = x.shape
    pos = jnp.arange(t, dtype=jnp.float32)
    h = x
    k_shared = None
    v_shared = None
    for i in range(DEPTH):
        if i < N_A_LAYERS:
            u = rms_norm(h, ret_norm_pre[i])
            proj = u @ ret_w_in[i]
            q, k, v, g = jnp.split(
                proj, [RET_QK_WIDTH, 2 * RET_QK_WIDTH, 2 * RET_QK_WIDTH + RET_WIDTH], axis=-1)
            q = rope(split_heads(q, RET_HEADS).astype(jnp.float32), pos)
            k = rope(split_heads(k, RET_HEADS).astype(jnp.float32), pos) * (RET_QK_DIM ** -0.5)
            v = split_heads(v, RET_HEADS).astype(jnp.float32)
            o = retention_chunkwise(q, k, v)
            mu = jnp.mean(o, axis=-1, keepdims=True)
            var = jnp.mean(jnp.square(o - mu), axis=-1, keepdims=True)
            o = (o - mu) * lax.rsqrt(var + EPS)
            o = merge_heads(o).astype(h.dtype) * jax.nn.silu(g)
            y = o @ ret_w_out[i]
            h = h + rms_norm(y, ret_norm_post[i])
            if i == N_A_LAYERS - 1:
                kv = rms_norm(h, kv_norm) @ w_kv
                k_sh, v_sh = jnp.split(kv, [SB_QK_WIDTH], axis=-1)
                k_shared = split_heads(k_sh, SB_HEADS).astype(jnp.float32)
                v_shared = split_heads(v_sh, SB_HEADS).astype(jnp.float32)
        else:
            j = i - N_A_LAYERS
            u = rms_norm(h, sb_norm_pre[j])
            proj = u @ sb_w_in[j]
            q, g = jnp.split(proj, [SB_QK_WIDTH], axis=-1)
            q = split_heads(q, SB_HEADS).astype(jnp.float32)
            o = stick_breaking_attention(q, k_shared, v_shared)
            o = merge_heads(o).astype(h.dtype) * jax.nn.silu(g)
            y = o @ sb_w_out[j]
            h = h + rms_norm(y, sb_norm_post[j])
    return h
```

```python
import functools
import math

import jax
import jax.numpy as jnp
from jax import lax
from jax.experimental import pallas as pl
from jax.experimental.pallas import tpu as pltpu

EPS = 1e-6
ROPE_BASE = 10000.0

RET_HEADS = 4
RET_QK_DIM = 256
RET_V_DIM = 512
RET_CHUNK = 256

SB_HEADS = 8
SB_QK_DIM = 128
SB_V_DIM = 256

VMEM_LIMIT_BYTES = 56 * 1024 * 1024

F32 = jnp.float32
BF16 = jnp.bfloat16


def _params(semantics):
    return pltpu.CompilerParams(dimension_semantics=semantics,
                                vmem_limit_bytes=VMEM_LIMIT_BYTES)


def _rms_scale(x):
    return x * lax.rsqrt(jnp.mean(x * x, axis=-1, keepdims=True) + EPS)


def _in_proj_kernel(x_ref, gain_ref, w_ref, cos_ref, sin_ref, o_ref, u_ref, *,
                    n_rope_groups, n_q_groups, k_scale, sub_n):
    j = pl.program_id(1)

    @pl.when(j == 0)
    def _():
        u_ref[...] = (_rms_scale(x_ref[...]) * gain_ref[...]).astype(BF16)

    tn = o_ref.shape[1]

    @pl.when(j == 0)
    def _():
        cos = cos_ref[...]
        sin = sin_ref[...]
        half = RET_QK_DIM // 2
        for grp in range(n_rope_groups):
            c0 = grp * RET_QK_DIM
            acc = jnp.dot(u_ref[...], w_ref[:, c0:c0 + RET_QK_DIM],
                          preferred_element_type=F32)
            if grp >= n_q_groups:
                acc = acc * k_scale
            x1 = acc[:, :half]
            x2 = acc[:, half:]
            o_ref[:, c0:c0 + half] = (x1 * cos - x2 * sin).astype(o_ref.dtype)
            o_ref[:, c0 + half:c0 + RET_QK_DIM] = (x1 * sin + x2 * cos).astype(o_ref.dtype)

    @pl.when(j != 0)
    def _():
        for c0 in range(0, tn, sub_n):
            o_ref[:, c0:c0 + sub_n] = jnp.dot(
                u_ref[...], w_ref[:, c0:c0 + sub_n],
                preferred_element_type=F32).astype(o_ref.dtype)


def _in_proj(x2d, gain, w, cos, sin, *, seq, tm=1024, tn=2048):
    n, d = x2d.shape
    n_out = w.shape[1]
    n_rope_groups = 2 * RET_HEADS
    assert n_rope_groups * RET_QK_DIM == tn and n % tm == 0 and seq % tm == 0 and n_out % tn == 0
    pos_blocks = seq // tm
    kern = functools.partial(_in_proj_kernel, n_rope_groups=n_rope_groups,
                             n_q_groups=RET_HEADS, k_scale=RET_QK_DIM ** -0.5, sub_n=512)
    return pl.pallas_call(
        kern,
        grid=(n // tm, n_out // tn),
        in_specs=[
            pl.BlockSpec((tm, d), lambda i, j: (i, 0)),
            pl.BlockSpec((1, d), lambda i, j: (0, 0)),
            pl.BlockSpec((d, tn), lambda i, j: (0, j)),
            pl.BlockSpec((tm, RET_QK_DIM // 2), lambda i, j: (i % pos_blocks, 0)),
            pl.BlockSpec((tm, RET_QK_DIM // 2), lambda i, j: (i % pos_blocks, 0)),
        ],
        out_specs=pl.BlockSpec((tm, tn), lambda i, j: (i, j)),
        out_shape=jax.ShapeDtypeStruct((n, n_out), BF16),
        scratch_shapes=[pltpu.VMEM((tm, d), BF16)],
        compiler_params=_params(("parallel", "arbitrary")),
        name="ret_in_proj",
    )(x2d, gain, w, cos, sin)


def _retention_kernel(q_ref, k_ref, v_ref, g_ref, dmask_ref, qdec_ref, kdec_ref, cdec_ref,
                      o_ref, state_ref, *, chunks_per_step):
    t = pl.program_id(1)

    @pl.when(t == 0)
    def _():
        state_ref[...] = jnp.zeros_like(state_ref)

    c = RET_CHUNK
    for ci in range(chunks_per_step):
        r0 = ci * c
        for h in range(RET_HEADS):
            q = q_ref[r0:r0 + c, h * RET_QK_DIM:(h + 1) * RET_QK_DIM]
            k = k_ref[r0:r0 + c, h * RET_QK_DIM:(h + 1) * RET_QK_DIM]
            v = v_ref[r0:r0 + c, h * RET_V_DIM:(h + 1) * RET_V_DIM]
            g = g_ref[r0:r0 + c, h * RET_V_DIM:(h + 1) * RET_V_DIM].astype(F32)
            state = state_ref[h]
            scores = lax.dot_general(q, k, (((1,), (1,)), ((), ())),
                                     preferred_element_type=F32) * dmask_ref[h]
            o = jnp.dot(scores.astype(BF16), v, preferred_element_type=F32)
            o = o + jnp.dot(q, state.astype(BF16), preferred_element_type=F32) * qdec_ref[h]
            v_dec = (v.astype(F32) * kdec_ref[h]).astype(BF16)
            state_ref[h] = state * cdec_ref[h] + lax.dot_general(
                k, v_dec, (((0,), (0,)), ((), ())), preferred_element_type=F32)
            mu = jnp.mean(o, axis=-1, keepdims=True)
            d = o - mu
            var = jnp.mean(d * d, axis=-1, keepdims=True)
            on = d * lax.rsqrt(var + EPS)
            gate = g / (1.0 + jnp.exp(-g))
            o_ref[r0:r0 + c, h * RET_V_DIM:(h + 1) * RET_V_DIM] = (on * gate).astype(o_ref.dtype)


def _retention(proj, tables, *, batch, seq, rows=512):
    n = proj.shape[0]
    qw = RET_HEADS * RET_QK_DIM
    vw = RET_HEADS * RET_V_DIM
    assert proj.shape[1] == 2 * qw + 2 * vw and vw == 2 * qw
    assert seq % rows == 0 and rows % RET_CHUNK == 0
    steps = seq // rows
    dmask, qdec, kdec, cdec = tables
    kern = functools.partial(_retention_kernel, chunks_per_step=rows // RET_CHUNK)
    row = lambda b, t: b * steps + t
    whole = lambda a: pl.BlockSpec(a.shape, lambda b, t: (0,) * a.ndim)
    return pl.pallas_call(
        kern,
        grid=(batch, steps),
        in_specs=[
            pl.BlockSpec((rows, qw), lambda b, t: (row(b, t), 0)),
            pl.BlockSpec((rows, qw), lambda b, t: (row(b, t), 1)),
            pl.BlockSpec((rows, vw), lambda b, t: (row(b, t), 1)),
            pl.BlockSpec((rows, vw), lambda b, t: (row(b, t), 2)),
            whole(dmask), whole(qdec), whole(kdec), whole(cdec),
        ],
        out_specs=pl.BlockSpec((rows, vw), lambda b, t: (row(b, t), 0)),
        out_shape=jax.ShapeDtypeStruct((n, vw), BF16),
        scratch_shapes=[pltpu.VMEM((RET_HEADS, RET_QK_DIM, RET_V_DIM), F32)],
        compiler_params=_params(("parallel", "arbitrary")),
        name="retention",
    )(proj, proj, proj, proj, dmask, qdec, kdec, cdec)


def _retention_tables():
    c = RET_CHUNK
    log_gamma = jnp.log1p(-jnp.exp2(-5.0 - jnp.arange(RET_HEADS, dtype=F32)))
    idx = jnp.arange(c, dtype=F32)
    diff = idx[:, None] - idx[None, :]
    dmask = jnp.where(diff >= 0, jnp.exp(log_gamma[:, None, None] * diff), 0.0)
    qdec = jnp.exp(log_gamma[:, None] * (idx[None, :] + 1.0))[:, :, None]
    kdec = jnp.exp(log_gamma[:, None] * (c - 1.0 - idx[None, :]))[:, :, None]
    cdec = jnp.broadcast_to(jnp.exp(log_gamma * c)[:, None, None], (RET_HEADS, 1, RET_V_DIM))
    return dmask, qdec, kdec, cdec


def _out_proj_kernel(o_ref, w_ref, gain_ref, res_ref, *rest, n_next):
    next_gain_refs = rest[:n_next]
    h_ref = rest[n_next]
    u_refs = rest[n_next + 1:]
    y = jnp.dot(o_ref[...], w_ref[...], preferred_element_type=F32)
    h = res_ref[...] + _rms_scale(y) * gain_ref[...]
    h_ref[...] = h
    if n_next:
        hn = _rms_scale(h)
        for gr, ur in zip(next_gain_refs, u_refs):
            ur[...] = (hn * gr[...]).astype(ur.dtype)


def _out_proj(o, w, gain, res, next_gains, *, tm=512):
    n, kdim = o.shape
    d = w.shape[1]
    n_next = len(next_gains)
    row_spec = lambda width: pl.BlockSpec((tm, width), lambda i: (i, 0))
    vec_spec = pl.BlockSpec((1, d), lambda i: (0, 0))
    outs = pl.pallas_call(
        functools.partial(_out_proj_kernel, n_next=n_next),
        grid=(n // tm,),
        in_specs=[row_spec(kdim), pl.BlockSpec((kdim, d), lambda i: (0, 0)), vec_spec,
                  row_spec(d)] + [vec_spec] * n_next,
        out_specs=[row_spec(d)] + [row_spec(d)] * n_next,
        out_shape=[jax.ShapeDtypeStruct((n, d), F32)]
        + [jax.ShapeDtypeStruct((n, d), BF16)] * n_next,
        compiler_params=_params(("parallel",)),
        name="out_proj_%d" % n_next,
    )(o, w, gain, res, *next_gains)
    return outs


def _matmul_kernel(a_ref, b_ref, o_ref, *, sub_n):
    for c0 in range(0, o_ref.shape[1], sub_n):
        o_ref[:, c0:c0 + sub_n] = jnp.dot(a_ref[...], b_ref[:, c0:c0 + sub_n],
                                          preferred_element_type=F32).astype(o_ref.dtype)


def _matmul(a, b, *, name, tm=1024, tn=1536):
    n, kdim = a.shape
    n_out = b.shape[1]
    assert n % tm == 0 and n_out % tn == 0
    return pl.pallas_call(
        functools.partial(_matmul_kernel, sub_n=512),
        grid=(n // tm, n_out // tn),
        in_specs=[pl.BlockSpec((tm, kdim), lambda i, j: (i, 0)),
                  pl.BlockSpec((kdim, tn), lambda i, j: (0, j))],
        out_specs=pl.BlockSpec((tm, tn), lambda i, j: (i, j)),
        out_shape=jax.ShapeDtypeStruct((n, n_out), BF16),
        compiler_params=_params(("parallel", "arbitrary")),
        name=name,
    )(a, b)


def _sb_kernel(q_ref, k_ref, v_ref, g_ref, o_ref, acc_ref, carry_ref, *, tq, tk, scale):
    i = pl.program_id(2)
    ratio = tq // tk
    q = q_ref[...]

    rr = lax.broadcasted_iota(jnp.int32, (tk, tk), 0)
    cc = lax.broadcasted_iota(jnp.int32, (tk, tk), 1)
    suffix_mat = (rr > cc).astype(BF16)

    acc_ref[...] = jnp.zeros_like(acc_ref)
    carry_ref[...] = jnp.zeros_like(carry_ref)

    def sweep(j, masked):
        start = pl.multiple_of(j * tk, tk)
        kb = k_ref[pl.ds(start, tk), :]
        vb = v_ref[pl.ds(start, tk), :]
        z = lax.dot_general(q, kb, (((1,), (1,)), ((), ())),
                            preferred_element_type=F32) * scale
        lneg = -(jnp.maximum(z, 0.0) + jnp.log(1.0 + jnp.exp(-jnp.abs(z))))
        if masked:
            t_idx = i * tq + lax.broadcasted_iota(jnp.int32, (tq, tk), 0)
            s_idx = j * tk + lax.broadcasted_iota(jnp.int32, (tq, tk), 1)
            causal = s_idx < t_idx
            lneg = jnp.where(causal, lneg, 0.0)
        lpos = lneg + z
        hi = lneg.astype(BF16)
        lo = (lneg - hi.astype(F32)).astype(BF16)
        suffix = (jnp.dot(hi, suffix_mat, preferred_element_type=F32)
                  + jnp.dot(lo, suffix_mat, preferred_element_type=F32))
        carry = carry_ref[...]
        w = jnp.exp(lpos + suffix + carry)
        if masked:
            w = jnp.where(causal, w, 0.0)
        acc_ref[...] += jnp.dot(w.astype(BF16), vb, preferred_element_type=F32)
        carry_ref[...] = carry + suffix[:, 0:1] + lneg[:, 0:1]

    for dj in range(ratio):
        sweep(i * ratio + (ratio - 1 - dj), True)

    def body(jj, c):
        sweep(i * ratio - 1 - jj, False)
        return c

    lax.fori_loop(0, i * ratio, body, 0)

    g = g_ref[...].astype(F32)
    o_ref[...] = (acc_ref[...] * (g / (1.0 + jnp.exp(-g)))).astype(o_ref.dtype)


def _sb_attention(qg, kv, *, batch, seq, tq=512, tk=256):
    n = qg.shape[0]
    qkw = SB_HEADS * SB_QK_DIM
    vw = SB_HEADS * SB_V_DIM
    assert qg.shape[1] == qkw + vw and kv.shape[1] == qkw + vw and vw == 2 * qkw
    nq = seq // tq
    v_col0 = qkw // SB_V_DIM
    kern = functools.partial(_sb_kernel, tq=tq, tk=tk, scale=1.0 / math.sqrt(SB_QK_DIM))
    return pl.pallas_call(
        kern,
        grid=(batch, SB_HEADS, nq),
        in_specs=[
            pl.BlockSpec((tq, SB_QK_DIM), lambda b, h, i: (b * nq + i, h)),
            pl.BlockSpec((seq, SB_QK_DIM), lambda b, h, i: (b, h)),
            pl.BlockSpec((seq, SB_V_DIM), lambda b, h, i: (b, v_col0 + h)),
            pl.BlockSpec((tq, SB_V_DIM), lambda b, h, i: (b * nq + i, v_col0 + h)),
        ],
        out_specs=pl.BlockSpec((tq, SB_V_DIM), lambda b, h, i: (b * nq + i, h)),
        out_shape=jax.ShapeDtypeStruct((n, vw), BF16),
        scratch_shapes=[pltpu.VMEM((tq, SB_V_DIM), F32), pltpu.VMEM((tq, 1), F32)],
        compiler_params=_params(("parallel", "parallel", "arbitrary")),
        name="sb_attention",
    )(qg, kv, kv, qg)


def kernel(x, ret_norm_pre, ret_w_in, ret_w_out, ret_norm_post, kv_norm, w_kv,
           sb_norm_pre, sb_w_in, sb_w_out, sb_norm_post):
    batch, seq, d = x.shape
    assert ret_w_in.shape[0] == 1 and sb_w_in.shape[0] == 1
    n = batch * seq
    x2d = x.reshape(n, d)

    pos = jnp.arange(seq, dtype=F32)
    half = RET_QK_DIM // 2
    inv_freq = 1.0 / (ROPE_BASE ** (jnp.arange(half, dtype=F32) / half))
    ang = pos[:, None] * inv_freq[None, :]
    cos, sin = jnp.cos(ang), jnp.sin(ang)

    vec = lambda g: g.reshape(1, d).astype(F32)

    proj = _in_proj(x2d, vec(ret_norm_pre[0]), ret_w_in[0].astype(BF16), cos, sin, seq=seq)
    o = _retention(proj, _retention_tables(), batch=batch, seq=seq)
    h1, u_kv, u_sb = _out_proj(o, ret_w_out[0].astype(BF16), vec(ret_norm_post[0]), x2d,
                               [vec(kv_norm), vec(sb_norm_pre[0])])

    kv = _matmul(u_kv, w_kv.astype(BF16), name="kv_proj")
    qg = _matmul(u_sb, sb_w_in[0].astype(BF16), name="sb_in_proj")
    o2 = _sb_attention(qg, kv, batch=batch, seq=seq)
    (out,) = _out_proj(o2, sb_w_out[0].astype(BF16), vec(sb_norm_post[0]), h1, [])
    return out.reshape(batch, seq, d)
```

```python
import functools
import math

import jax
import jax.numpy as jnp
from jax import lax
from jax.experimental import pallas as pl
from jax.experimental.pallas import tpu as pltpu

EPS = 1e-6
ROPE_BASE = 10000.0

RET_HEADS = 4
RET_QK_DIM = 256
RET_V_DIM = 512
RET_CHUNK = 256

SB_HEADS = 8
SB_QK_DIM = 128
SB_V_DIM = 256

VMEM_LIMIT_BYTES = 56 * 1024 * 1024

MASKED_LOG2_WEIGHT = -1e30

F32 = jnp.float32
BF16 = jnp.bfloat16


def _params(semantics):
    return pltpu.CompilerParams(dimension_semantics=semantics,
                                vmem_limit_bytes=VMEM_LIMIT_BYTES)


def _rms_scale(x):
    return x * lax.rsqrt(jnp.mean(x * x, axis=-1, keepdims=True) + EPS)


def _in_proj_kernel(x_ref, gain_ref, w_ref, cos_ref, sin_ref, o_ref, u_ref, *,
                    n_rope_groups, n_q_groups, k_scale, sub_n):
    j = pl.program_id(1)

    @pl.when(j == 0)
    def _():
        u_ref[...] = (_rms_scale(x_ref[...]) * gain_ref[...]).astype(BF16)

    tn = o_ref.shape[1]

    @pl.when(j == 0)
    def _():
        cos = cos_ref[...]
        sin = sin_ref[...]
        half = RET_QK_DIM // 2
        for grp in range(n_rope_groups):
            c0 = grp * RET_QK_DIM
            acc = jnp.dot(u_ref[...], w_ref[:, c0:c0 + RET_QK_DIM],
                          preferred_element_type=F32)
            if grp >= n_q_groups:
                acc = acc * k_scale
            x1 = acc[:, :half]
            x2 = acc[:, half:]
            o_ref[:, c0:c0 + half] = (x1 * cos - x2 * sin).astype(o_ref.dtype)
            o_ref[:, c0 + half:c0 + RET_QK_DIM] = (x1 * sin + x2 * cos).astype(o_ref.dtype)

    @pl.when(j != 0)
    def _():
        for c0 in range(0, tn, sub_n):
            o_ref[:, c0:c0 + sub_n] = jnp.dot(
                u_ref[...], w_ref[:, c0:c0 + sub_n],
                preferred_element_type=F32).astype(o_ref.dtype)


def _in_proj(x2d, gain, w, cos, sin, *, seq, tm=1024, tn=2048):
    n, d = x2d.shape
    n_out = w.shape[1]
    n_rope_groups = 2 * RET_HEADS
    assert n_rope_groups * RET_QK_DIM == tn and n % tm == 0 and seq % tm == 0 and n_out % tn == 0
    pos_blocks = seq // tm
    kern = functools.partial(_in_proj_kernel, n_rope_groups=n_rope_groups,
                             n_q_groups=RET_HEADS, k_scale=RET_QK_DIM ** -0.5, sub_n=512)
    return pl.pallas_call(
        kern,
        grid=(n // tm, n_out // tn),
        in_specs=[
            pl.BlockSpec((tm, d), lambda i, j: (i, 0)),
            pl.BlockSpec((1, d), lambda i, j: (0, 0)),
            pl.BlockSpec((d, tn), lambda i, j: (0, j)),
            pl.BlockSpec((tm, RET_QK_DIM // 2), lambda i, j: (i % pos_blocks, 0)),
            pl.BlockSpec((tm, RET_QK_DIM // 2), lambda i, j: (i % pos_blocks, 0)),
        ],
        out_specs=pl.BlockSpec((tm, tn), lambda i, j: (i, j)),
        out_shape=jax.ShapeDtypeStruct((n, n_out), BF16),
        scratch_shapes=[pltpu.VMEM((tm, d), BF16)],
        compiler_params=_params(("parallel", "arbitrary")),
        name="ret_in_proj",
    )(x2d, gain, w, cos, sin)


def _retention_kernel(q_ref, k_ref, v_ref, g_ref, dmask_ref, qdec_ref, kdec_ref, cdec_ref,
                      o_ref, state_ref, *, chunks_per_step):
    t = pl.program_id(1)

    @pl.when(t == 0)
    def _():
        state_ref[...] = jnp.zeros_like(state_ref)

    c = RET_CHUNK
    for ci in range(chunks_per_step):
        r0 = ci * c
        for h in range(RET_HEADS):
            q = q_ref[r0:r0 + c, h * RET_QK_DIM:(h + 1) * RET_QK_DIM]
            k = k_ref[r0:r0 + c, h * RET_QK_DIM:(h + 1) * RET_QK_DIM]
            v = v_ref[r0:r0 + c, h * RET_V_DIM:(h + 1) * RET_V_DIM]
            g = g_ref[r0:r0 + c, h * RET_V_DIM:(h + 1) * RET_V_DIM].astype(F32)
            state = state_ref[h]
            scores = lax.dot_general(q, k, (((1,), (1,)), ((), ())),
                                     preferred_element_type=F32) * dmask_ref[h]
            o = jnp.dot(scores.astype(BF16), v, preferred_element_type=F32)
            o = o + jnp.dot(q, state.astype(BF16), preferred_element_type=F32) * qdec_ref[h]
            v_dec = (v.astype(F32) * kdec_ref[h]).astype(BF16)
            state_ref[h] = state * cdec_ref[h] + lax.dot_general(
                k, v_dec, (((0,), (0,)), ((), ())), preferred_element_type=F32)
            mu = jnp.mean(o, axis=-1, keepdims=True)
            d = o - mu
            var = jnp.mean(d * d, axis=-1, keepdims=True)
            on = d * lax.rsqrt(var + EPS)
            gate = g / (1.0 + jnp.exp(-g))
            o_ref[r0:r0 + c, h * RET_V_DIM:(h + 1) * RET_V_DIM] = (on * gate).astype(o_ref.dtype)


def _retention(proj, tables, *, batch, seq, rows=512):
    n = proj.shape[0]
    qw = RET_HEADS * RET_QK_DIM
    vw = RET_HEADS * RET_V_DIM
    assert proj.shape[1] == 2 * qw + 2 * vw and vw == 2 * qw
    assert seq % rows == 0 and rows % RET_CHUNK == 0
    steps = seq // rows
    dmask, qdec, kdec, cdec = tables
    kern = functools.partial(_retention_kernel, chunks_per_step=rows // RET_CHUNK)
    row = lambda b, t: b * steps + t
    whole = lambda a: pl.BlockSpec(a.shape, lambda b, t: (0,) * a.ndim)
    return pl.pallas_call(
        kern,
        grid=(batch, steps),
        in_specs=[
            pl.BlockSpec((rows, qw), lambda b, t: (row(b, t), 0)),
            pl.BlockSpec((rows, qw), lambda b, t: (row(b, t), 1)),
            pl.BlockSpec((rows, vw), lambda b, t: (row(b, t), 1)),
            pl.BlockSpec((rows, vw), lambda b, t: (row(b, t), 2)),
            whole(dmask), whole(qdec), whole(kdec), whole(cdec),
        ],
        out_specs=pl.BlockSpec((rows, vw), lambda b, t: (row(b, t), 0)),
        out_shape=jax.ShapeDtypeStruct((n, vw), BF16),
        scratch_shapes=[pltpu.VMEM((RET_HEADS, RET_QK_DIM, RET_V_DIM), F32)],
        compiler_params=_params(("parallel", "arbitrary")),
        name="retention",
    )(proj, proj, proj, proj, dmask, qdec, kdec, cdec)


def _retention_tables():
    c = RET_CHUNK
    log_gamma = jnp.log1p(-jnp.exp2(-5.0 - jnp.arange(RET_HEADS, dtype=F32)))
    idx = jnp.arange(c, dtype=F32)
    diff = idx[:, None] - idx[None, :]
    dmask = jnp.where(diff >= 0, jnp.exp(log_gamma[:, None, None] * diff), 0.0)
    qdec = jnp.exp(log_gamma[:, None] * (idx[None, :] + 1.0))[:, :, None]
    kdec = jnp.exp(log_gamma[:, None] * (c - 1.0 - idx[None, :]))[:, :, None]
    cdec = jnp.broadcast_to(jnp.exp(log_gamma * c)[:, None, None], (RET_HEADS, 1, RET_V_DIM))
    return dmask, qdec, kdec, cdec


def _out_proj_kernel(o_ref, w_ref, gain_ref, res_ref, *rest, n_next):
    next_gain_refs = rest[:n_next]
    h_ref = rest[n_next]
    u_refs = rest[n_next + 1:]
    y = jnp.dot(o_ref[...], w_ref[...], preferred_element_type=F32)
    h = res_ref[...] + _rms_scale(y) * gain_ref[...]
    h_ref[...] = h
    if n_next:
        hn = _rms_scale(h)
        for gr, ur in zip(next_gain_refs, u_refs):
            ur[...] = (hn * gr[...]).astype(ur.dtype)


def _out_proj(o, w, gain, res, next_gains, *, tm=512):
    n, kdim = o.shape
    d = w.shape[1]
    n_next = len(next_gains)
    row_spec = lambda width: pl.BlockSpec((tm, width), lambda i: (i, 0))
    vec_spec = pl.BlockSpec((1, d), lambda i: (0, 0))
    outs = pl.pallas_call(
        functools.partial(_out_proj_kernel, n_next=n_next),
        grid=(n // tm,),
        in_specs=[row_spec(kdim), pl.BlockSpec((kdim, d), lambda i: (0, 0)), vec_spec,
                  row_spec(d)] + [vec_spec] * n_next,
        out_specs=[row_spec(d)] + [row_spec(d)] * n_next,
        out_shape=[jax.ShapeDtypeStruct((n, d), F32)]
        + [jax.ShapeDtypeStruct((n, d), BF16)] * n_next,
        compiler_params=_params(("parallel",)),
        name="out_proj_%d" % n_next,
    )(o, w, gain, res, *next_gains)
    return outs


def _matmul_kernel(a_ref, b_ref, o_ref, *, sub_n):
    for c0 in range(0, o_ref.shape[1], sub_n):
        o_ref[:, c0:c0 + sub_n] = jnp.dot(a_ref[...], b_ref[:, c0:c0 + sub_n],
                                          preferred_element_type=F32).astype(o_ref.dtype)


def _matmul(a, b, *, name, tm=1024, tn=1536):
    n, kdim = a.shape
    n_out = b.shape[1]
    assert n % tm == 0 and n_out % tn == 0
    return pl.pallas_call(
        functools.partial(_matmul_kernel, sub_n=512),
        grid=(n // tm, n_out // tn),
        in_specs=[pl.BlockSpec((tm, kdim), lambda i, j: (i, 0)),
                  pl.BlockSpec((kdim, tn), lambda i, j: (0, j))],
        out_specs=pl.BlockSpec((tm, tn), lambda i, j: (i, j)),
        out_shape=jax.ShapeDtypeStruct((n, n_out), BF16),
        compiler_params=_params(("parallel", "arbitrary")),
        name=name,
    )(a, b)


def _sb_kernel(q_ref, k_ref, v_ref, g_ref, mm_ref, o_ref,
               hl0_ref, hl1_ref, zp0_ref, zp1_ref, zbuf_ref, acc_ref, carry_ref, *,
               tq, tk, scale2):
    assert tq == 2 * tk
    i = pl.program_id(2)
    slots = ((hl0_ref, zp0_ref), (hl1_ref, zp1_ref))

    acc_ref[...] = jnp.zeros_like(acc_ref)
    carry_ref[...] = jnp.zeros_like(carry_ref)

    def qk(j):
        kb = k_ref[pl.ds(pl.multiple_of(jnp.maximum(j, 0) * tk, tk), tk), :]
        return lax.dot_general(q_ref[...], kb, (((1,), (1,)), ((), ())),
                               preferred_element_type=F32)

    def score(z, j, slot, masked):
        hl_ref, zp_ref = slots[slot]
        z2 = z * scale2
        p = jnp.maximum(z2, 0.0) + jnp.log2(1.0 + jnp.exp2(-jnp.abs(z2)))
        zp = z2 - p
        if masked:
            t_idx = i * tq + lax.broadcasted_iota(jnp.int32, (tq, tk), 0)
            s_idx = j * tk + lax.broadcasted_iota(jnp.int32, (tq, tk), 1)
            causal = s_idx < t_idx
            p = jnp.where(causal, p, 0.0)
            zp = jnp.where(causal, zp, MASKED_LOG2_WEIGHT)
        hi = p.astype(BF16)
        hl_ref[:, :tk] = hi
        hl_ref[:, tk:] = (p - hi.astype(F32)).astype(BF16)
        zp_ref[...] = zp

    def suffix(slot):
        return jnp.dot(slots[slot][0][...], mm_ref[...], preferred_element_type=F32)

    def finish(j, slot, sfx):
        hl_ref, zp_ref = slots[slot]
        vb = v_ref[pl.ds(pl.multiple_of(j * tk, tk), tk), :]
        carry = carry_ref[...]
        w = jnp.exp2(zp_ref[...] - sfx - carry)
        acc_ref[...] += jnp.dot(w.astype(BF16), vb, preferred_element_type=F32)
        p_first = hl_ref[:, 0:1].astype(F32) + hl_ref[:, tk:tk + 1].astype(F32)
        carry_ref[...] = carry + sfx[:, 0:1] + p_first

    top = 2 * i + 1
    score(qk(top), top, 0, True)
    sfx = suffix(0)
    z_next = qk(top - 1)
    zbuf_ref[...] = qk(top - 2)
    score(z_next, top - 1, 1, True)
    finish(top, 0, sfx)

    def pair(p, c):
        j = 2 * i - 1 - 2 * p
        sfx_a = suffix(1)
        z_b = qk(j - 1)
        score(zbuf_ref[...], j, 0, False)
        finish(j + 1, 1, sfx_a)
        score(z_b, j - 1, 1, False)
        sfx_c = suffix(0)
        zbuf_ref[...] = qk(j - 2)
        finish(j, 0, sfx_c)
        return c

    lax.fori_loop(0, i, pair, 0)
    finish(0, 1, suffix(1))

    g = g_ref[...].astype(F32)
    o_ref[...] = (acc_ref[...] * (g / (1.0 + jnp.exp(-g)))).astype(o_ref.dtype)


def _sb_attention(qg, kv, *, batch, seq, tq=512, tk=256):
    n = qg.shape[0]
    qkw = SB_HEADS * SB_QK_DIM
    vw = SB_HEADS * SB_V_DIM
    assert qg.shape[1] == qkw + vw and kv.shape[1] == qkw + vw and vw == 2 * qkw
    assert seq % tq == 0
    nq = seq // tq
    v_col0 = qkw // SB_V_DIM
    rr = lax.broadcasted_iota(jnp.int32, (2 * tk, tk), 0) % tk
    cc = lax.broadcasted_iota(jnp.int32, (2 * tk, tk), 1)
    suffix_mat = (rr > cc).astype(BF16)
    kern = functools.partial(_sb_kernel, tq=tq, tk=tk,
                             scale2=math.log2(math.e) / math.sqrt(SB_QK_DIM))
    return pl.pallas_call(
        kern,
        grid=(batch, SB_HEADS, nq),
        in_specs=[
            pl.BlockSpec((tq, SB_QK_DIM), lambda b, h, i: (b * nq + i, h)),
            pl.BlockSpec((seq, SB_QK_DIM), lambda b, h, i: (b, h)),
            pl.BlockSpec((seq, SB_V_DIM), lambda b, h, i: (b, v_col0 + h)),
            pl.BlockSpec((tq, SB_V_DIM), lambda b, h, i: (b * nq + i, v_col0 + h)),
            pl.BlockSpec((2 * tk, tk), lambda b, h, i: (0, 0)),
        ],
        out_specs=pl.BlockSpec((tq, SB_V_DIM), lambda b, h, i: (b * nq + i, h)),
        out_shape=jax.ShapeDtypeStruct((n, vw), BF16),
        scratch_shapes=[pltpu.VMEM((tq, 2 * tk), BF16), pltpu.VMEM((tq, 2 * tk), BF16),
                        pltpu.VMEM((tq, tk), F32), pltpu.VMEM((tq, tk), F32),
                        pltpu.VMEM((tq, tk), F32),
                        pltpu.VMEM((tq, SB_V_DIM), F32), pltpu.VMEM((tq, 1), F32)],
        compiler_params=_params(("parallel", "parallel", "arbitrary")),
        name="sb_attention",
    )(qg, kv, kv, qg, suffix_mat)


def kernel(x, ret_norm_pre, ret_w_in, ret_w_out, ret_norm_post, kv_norm, w_kv,
           sb_norm_pre, sb_w_in, sb_w_out, sb_norm_post):
    batch, seq, d = x.shape
    assert ret_w_in.shape[0] == 1 and sb_w_in.shape[0] == 1
    n = batch * seq
    x2d = x.reshape(n, d)

    pos = jnp.arange(seq, dtype=F32)
    half = RET_QK_DIM // 2
    inv_freq = 1.0 / (ROPE_BASE ** (jnp.arange(half, dtype=F32) / half))
    ang = pos[:, None] * inv_freq[None, :]
    cos, sin = jnp.cos(ang), jnp.sin(ang)

    vec = lambda g: g.reshape(1, d).astype(F32)

    proj = _in_proj(x2d, vec(ret_norm_pre[0]), ret_w_in[0].astype(BF16), cos, sin, seq=seq)
    o = _retention(proj, _retention_tables(), batch=batch, seq=seq)
    h1, u_kv, u_sb = _out_proj(o, ret_w_out[0].astype(BF16), vec(ret_norm_post[0]), x2d,
                               [vec(kv_norm), vec(sb_norm_pre[0])])

    kv = _matmul(u_kv, w_kv.astype(BF16), name="kv_proj")
    qg = _matmul(u_sb, sb_w_in[0].astype(BF16), name="sb_in_proj")
    o2 = _sb_attention(qg, kv, batch=batch, seq=seq)
    (out,) = _out_proj(o2, sb_w_out[0].astype(BF16), vec(sb_norm_post[0]), h1, [])
    return out.reshape(batch, seq, d)
```

```python
import functools
import math

import jax
import jax.numpy as jnp
from jax import lax
from jax.experimental import pallas as pl
from jax.experimental.pallas import tpu as pltpu

EPS = 1e-6
ROPE_BASE = 10000.0

RET_HEADS = 4
RET_QK_DIM = 256
RET_V_DIM = 512
RET_CHUNK = 256

SB_HEADS = 8
SB_QK_DIM = 128
SB_V_DIM = 256

VMEM_LIMIT_BYTES = 56 * 1024 * 1024

MASKED_LOG2_WEIGHT = -1e30

F32 = jnp.float32
BF16 = jnp.bfloat16


def _params(semantics):
    return pltpu.CompilerParams(dimension_semantics=semantics,
                                vmem_limit_bytes=VMEM_LIMIT_BYTES)


def _rms_scale(x):
    return x * lax.rsqrt(jnp.mean(x * x, axis=-1, keepdims=True) + EPS)


def _in_proj_kernel(x_ref, gain_ref, w_ref, cos_ref, sin_ref, o_ref, u_ref, *,
                    n_rope_groups, n_q_groups, k_scale, sub_n):
    j = pl.program_id(1)

    @pl.when(j == 0)
    def _():
        u_ref[...] = (_rms_scale(x_ref[...]) * gain_ref[...]).astype(BF16)

    tn = o_ref.shape[1]

    @pl.when(j == 0)
    def _():
        cos = cos_ref[...]
        sin = sin_ref[...]
        half = RET_QK_DIM // 2
        for grp in range(n_rope_groups):
            c0 = grp * RET_QK_DIM
            acc = jnp.dot(u_ref[...], w_ref[:, c0:c0 + RET_QK_DIM],
                          preferred_element_type=F32)
            if grp >= n_q_groups:
                acc = acc * k_scale
            x1 = acc[:, :half]
            x2 = acc[:, half:]
            o_ref[:, c0:c0 + half] = (x1 * cos - x2 * sin).astype(o_ref.dtype)
            o_ref[:, c0 + half:c0 + RET_QK_DIM] = (x1 * sin + x2 * cos).astype(o_ref.dtype)

    @pl.when(j != 0)
    def _():
        for c0 in range(0, tn, sub_n):
            o_ref[:, c0:c0 + sub_n] = jnp.dot(
                u_ref[...], w_ref[:, c0:c0 + sub_n],
                preferred_element_type=F32).astype(o_ref.dtype)


def _in_proj(x2d, gain, w, cos, sin, *, seq, tm=1024, tn=2048):
    n, d = x2d.shape
    n_out = w.shape[1]
    n_rope_groups = 2 * RET_HEADS
    assert n_rope_groups * RET_QK_DIM == tn and n % tm == 0 and seq % tm == 0 and n_out % tn == 0
    pos_blocks = seq // tm
    kern = functools.partial(_in_proj_kernel, n_rope_groups=n_rope_groups,
                             n_q_groups=RET_HEADS, k_scale=RET_QK_DIM ** -0.5, sub_n=512)
    return pl.pallas_call(
        kern,
        grid=(n // tm, n_out // tn),
        in_specs=[
            pl.BlockSpec((tm, d), lambda i, j: (i, 0)),
            pl.BlockSpec((1, d), lambda i, j: (0, 0)),
            pl.BlockSpec((d, tn), lambda i, j: (0, j)),
            pl.BlockSpec((tm, RET_QK_DIM // 2), lambda i, j: (i % pos_blocks, 0)),
            pl.BlockSpec((tm, RET_QK_DIM // 2), lambda i, j: (i % pos_blocks, 0)),
        ],
        out_specs=pl.BlockSpec((tm, tn), lambda i, j: (i, j)),
        out_shape=jax.ShapeDtypeStruct((n, n_out), BF16),
        scratch_shapes=[pltpu.VMEM((tm, d), BF16)],
        compiler_params=_params(("parallel", "arbitrary")),
        name="ret_in_proj",
    )(x2d, gain, w, cos, sin)


def _retention_kernel(q_ref, k_ref, v_ref, g_ref, dmask_ref, qdec_ref, kdec_ref, cdec_ref,
                      o_ref, state_ref, *, chunks_per_step):
    t = pl.program_id(1)

    @pl.when(t == 0)
    def _():
        state_ref[...] = jnp.zeros_like(state_ref)

    c = RET_CHUNK
    for ci in range(chunks_per_step):
        r0 = ci * c
        for h in range(RET_HEADS):
            q = q_ref[r0:r0 + c, h * RET_QK_DIM:(h + 1) * RET_QK_DIM]
            k = k_ref[r0:r0 + c, h * RET_QK_DIM:(h + 1) * RET_QK_DIM]
            v = v_ref[r0:r0 + c, h * RET_V_DIM:(h + 1) * RET_V_DIM]
            g = g_ref[r0:r0 + c, h * RET_V_DIM:(h + 1) * RET_V_DIM].astype(F32)
            state = state_ref[h]
            scores = lax.dot_general(q, k, (((1,), (1,)), ((), ())),
                                     preferred_element_type=F32) * dmask_ref[h]
            o = jnp.dot(scores.astype(BF16), v, preferred_element_type=F32)
            o = o + jnp.dot(q, state.astype(BF16), preferred_element_type=F32) * qdec_ref[h]
            v_dec = (v.astype(F32) * kdec_ref[h]).astype(BF16)
            state_ref[h] = state * cdec_ref[h] + lax.dot_general(
                k, v_dec, (((0,), (0,)), ((), ())), preferred_element_type=F32)
            mu = jnp.mean(o, axis=-1, keepdims=True)
            d = o - mu
            var = jnp.mean(d * d, axis=-1, keepdims=True)
            on = d * lax.rsqrt(var + EPS)
            gate = g / (1.0 + jnp.exp(-g))
            o_ref[r0:r0 + c, h * RET_V_DIM:(h + 1) * RET_V_DIM] = (on * gate).astype(o_ref.dtype)


def _retention(proj, tables, *, batch, seq, rows=512):
    n = proj.shape[0]
    qw = RET_HEADS * RET_QK_DIM
    vw = RET_HEADS * RET_V_DIM
    assert proj.shape[1] == 2 * qw + 2 * vw and vw == 2 * qw
    assert seq % rows == 0 and rows % RET_CHUNK == 0
    steps = seq // rows
    dmask, qdec, kdec, cdec = tables
    kern = functools.partial(_retention_kernel, chunks_per_step=rows // RET_CHUNK)
    row = lambda b, t: b * steps + t
    whole = lambda a: pl.BlockSpec(a.shape, lambda b, t: (0,) * a.ndim)
    return pl.pallas_call(
        kern,
        grid=(batch, steps),
        in_specs=[
            pl.BlockSpec((rows, qw), lambda b, t: (row(b, t), 0)),
            pl.BlockSpec((rows, qw), lambda b, t: (row(b, t), 1)),
            pl.BlockSpec((rows, vw), lambda b, t: (row(b, t), 1)),
            pl.BlockSpec((rows, vw), lambda b, t: (row(b, t), 2)),
            whole(dmask), whole(qdec), whole(kdec), whole(cdec),
        ],
        out_specs=pl.BlockSpec((rows, vw), lambda b, t: (row(b, t), 0)),
        out_shape=jax.ShapeDtypeStruct((n, vw), BF16),
        scratch_shapes=[pltpu.VMEM((RET_HEADS, RET_QK_DIM, RET_V_DIM), F32)],
        compiler_params=_params(("parallel", "arbitrary")),
        name="retention",
    )(proj, proj, proj, proj, dmask, qdec, kdec, cdec)


def _retention_tables():
    c = RET_CHUNK
    log_gamma = jnp.log1p(-jnp.exp2(-5.0 - jnp.arange(RET_HEADS, dtype=F32)))
    idx = jnp.arange(c, dtype=F32)
    diff = idx[:, None] - idx[None, :]
    dmask = jnp.where(diff >= 0, jnp.exp(log_gamma[:, None, None] * diff), 0.0)
    qdec = jnp.exp(log_gamma[:, None] * (idx[None, :] + 1.0))[:, :, None]
    kdec = jnp.exp(log_gamma[:, None] * (c - 1.0 - idx[None, :]))[:, :, None]
    cdec = jnp.broadcast_to(jnp.exp(log_gamma * c)[:, None, None], (RET_HEADS, 1, RET_V_DIM))
    return dmask, qdec, kdec, cdec


def _out_proj_kernel(o_ref, w_ref, gain_ref, res_ref, *rest, n_next):
    next_gain_refs = rest[:n_next]
    h_ref = rest[n_next]
    u_refs = rest[n_next + 1:]
    y = jnp.dot(o_ref[...], w_ref[...], preferred_element_type=F32)
    h = res_ref[...] + _rms_scale(y) * gain_ref[...]
    h_ref[...] = h
    if n_next:
        hn = _rms_scale(h)
        for gr, ur in zip(next_gain_refs, u_refs):
            ur[...] = (hn * gr[...]).astype(ur.dtype)


def _out_proj(o, w, gain, res, next_gains, *, tm=512):
    n, kdim = o.shape
    d = w.shape[1]
    n_next = len(next_gains)
    row_spec = lambda width: pl.BlockSpec((tm, width), lambda i: (i, 0))
    vec_spec = pl.BlockSpec((1, d), lambda i: (0, 0))
    outs = pl.pallas_call(
        functools.partial(_out_proj_kernel, n_next=n_next),
        grid=(n // tm,),
        in_specs=[row_spec(kdim), pl.BlockSpec((kdim, d), lambda i: (0, 0)), vec_spec,
                  row_spec(d)] + [vec_spec] * n_next,
        out_specs=[row_spec(d)] + [row_spec(d)] * n_next,
        out_shape=[jax.ShapeDtypeStruct((n, d), F32)]
        + [jax.ShapeDtypeStruct((n, d), BF16)] * n_next,
        compiler_params=_params(("parallel",)),
        name="out_proj_%d" % n_next,
    )(o, w, gain, res, *next_gains)
    return outs


def _matmul_kernel(a_ref, b_ref, o_ref, *, sub_n):
    for c0 in range(0, o_ref.shape[1], sub_n):
        o_ref[:, c0:c0 + sub_n] = jnp.dot(a_ref[...], b_ref[:, c0:c0 + sub_n],
                                          preferred_element_type=F32).astype(o_ref.dtype)


def _matmul(a, b, *, name, tm=1024, tn=1536):
    n, kdim = a.shape
    n_out = b.shape[1]
    assert n % tm == 0 and n_out % tn == 0
    return pl.pallas_call(
        functools.partial(_matmul_kernel, sub_n=512),
        grid=(n // tm, n_out // tn),
        in_specs=[pl.BlockSpec((tm, kdim), lambda i, j: (i, 0)),
                  pl.BlockSpec((kdim, tn), lambda i, j: (0, j))],
        out_specs=pl.BlockSpec((tm, tn), lambda i, j: (i, j)),
        out_shape=jax.ShapeDtypeStruct((n, n_out), BF16),
        compiler_params=_params(("parallel", "arbitrary")),
        name=name,
    )(a, b)


def _sb_kernel(mq_ref, mj_ref, uq_ref, uj_ref, q_ref, k_ref, v_ref, g_ref, mm_ref, o_ref,
               z0_ref, z1_ref, pb0_ref, pb1_ref, zp0_ref, zp1_ref, x0_ref, x1_ref,
               p00_ref, p01_ref, rs0_ref, rs1_ref, acc_ref, carry_ref, *,
               tq, tk, scale2, n_masked, n_unmasked):
    z_refs = (z0_ref, z1_ref)
    pb_refs = (pb0_ref, pb1_ref)
    p0_refs = (p00_ref, p01_ref)
    zp_refs = (zp0_ref, zp1_ref)
    x_refs = (x0_ref, x1_ref)
    rs_refs = (rs0_ref, rs1_ref)

    def rows(qi):
        return pl.ds(pl.multiple_of(qi * tq, tq), tq)

    def keys(j):
        return pl.ds(pl.multiple_of(j * tk, tk), tk)

    def qk(qi, j, par):
        z_refs[par][...] = lax.dot_general(q_ref[rows(qi), :], k_ref[keys(j), :],
                                           (((1,), (1,)), ((), ())),
                                           preferred_element_type=F32)

    def score(qi, j, par, masked):
        z2 = z_refs[par][...] * scale2
        p = jnp.maximum(z2, 0.0) + jnp.log2(1.0 + jnp.exp2(-jnp.abs(z2)))
        zp = z2 - p
        if masked:
            col_minus_row = (lax.broadcasted_iota(jnp.int32, (tq, tk), 1)
                             - lax.broadcasted_iota(jnp.int32, (tq, tk), 0))
            causal = col_minus_row < qi * tq - j * tk
            p = jnp.where(causal, p, 0.0)
            zp = jnp.where(causal, zp, MASKED_LOG2_WEIGHT)
        pb_refs[par][...] = p.astype(BF16)
        p0_refs[par][...] = p[:, 0:1]
        zp_refs[par][...] = zp

    def suffix(par):
        sfx = jnp.dot(pb_refs[par][...], mm_ref[...], preferred_element_type=F32)
        x_refs[par][...] = zp_refs[par][...] - sfx
        rs_refs[par][...] = sfx[:, 0:1] + p0_refs[par][...]

    def finish(qi, j, par, first):
        x = x_refs[par][...]
        row_sum = rs_refs[par][...]
        if first:
            pv = jnp.dot(jnp.exp2(x).astype(BF16), v_ref[keys(j), :], preferred_element_type=F32)
            acc_ref[qi] = pv
            carry_ref[qi] = row_sum
        else:
            carry = carry_ref[qi]
            pv = jnp.dot(jnp.exp2(x - carry).astype(BF16), v_ref[keys(j), :],
                         preferred_element_type=F32)
            acc_ref[qi] += pv
            carry_ref[qi] = carry + row_sum

    def run(n_tasks, tq_ref, tj_ref, masked):
        assert n_tasks >= 4 and n_tasks % 2 == 0
        task = lambda n: (tq_ref[n], tj_ref[n])

        def iteration(n, par):
            is_static = isinstance(n, int)
            if not is_static or 0 <= n - 1 < n_tasks:
                score(*task(n - 1), 1 - par, masked)
            if not is_static or 0 <= n - 2 < n_tasks:
                suffix(par)
            if not is_static or n < n_tasks:
                qk(*task(n), par)
            if not is_static or 0 <= n - 3 < n_tasks:
                finish(*task(n - 3), 1 - par, first=masked and par == 1)

        for n in range(4):
            iteration(n, n % 2)

        def pair(m, c):
            n = 4 + 2 * m
            iteration(n, 0)
            iteration(n + 1, 1)
            return c

        lax.fori_loop(0, (n_tasks - 4) // 2, pair, 0)
        for n in range(n_tasks, n_tasks + 3):
            iteration(n, n % 2)

    run(n_masked, mq_ref, mj_ref, True)
    run(n_unmasked, uq_ref, uj_ref, False)

    for qi in range(acc_ref.shape[0]):
        g = g_ref[qi * tq:(qi + 1) * tq, :].astype(F32)
        o_ref[qi * tq:(qi + 1) * tq, :] = (acc_ref[qi] * (g / (1.0 + jnp.exp(-g)))).astype(o_ref.dtype)


def _sb_attention(qg, kv, *, batch, seq, tq=512, tk=256):
    n = qg.shape[0]
    qkw = SB_HEADS * SB_QK_DIM
    vw = SB_HEADS * SB_V_DIM
    assert qg.shape[1] == qkw + vw and kv.shape[1] == qkw + vw and vw == 2 * qkw
    assert seq % tq == 0 and tq == 2 * tk
    nq = seq // tq
    v_col0 = qkw // SB_V_DIM
    rr = lax.broadcasted_iota(jnp.int32, (tk, tk), 0)
    cc = lax.broadcasted_iota(jnp.int32, (tk, tk), 1)
    suffix_mat = (rr > cc).astype(BF16)
    masked = [(qi, j) for qi in range(nq) for j in (2 * qi + 1, 2 * qi)]
    unmasked = [(qi, j) for qi in range(nq) for j in range(2 * qi - 1, -1, -1)]
    tables = [jnp.asarray(col, jnp.int32)
              for col in (*zip(*masked), *zip(*unmasked))]
    kern = functools.partial(_sb_kernel, tq=tq, tk=tk,
                             scale2=math.log2(math.e) / math.sqrt(SB_QK_DIM),
                             n_masked=len(masked), n_unmasked=len(unmasked))
    stage_f32 = pltpu.VMEM((tq, tk), F32)
    stage_bf16 = pltpu.VMEM((tq, tk), BF16)
    stage_col = pltpu.VMEM((tq, 1), F32)
    return pl.pallas_call(
        kern,
        grid_spec=pltpu.PrefetchScalarGridSpec(
            num_scalar_prefetch=len(tables),
            grid=(batch, SB_HEADS),
            in_specs=[
                pl.BlockSpec((seq, SB_QK_DIM), lambda b, h, *_: (b, h)),
                pl.BlockSpec((seq, SB_QK_DIM), lambda b, h, *_: (b, h)),
                pl.BlockSpec((seq, SB_V_DIM), lambda b, h, *_: (b, v_col0 + h)),
                pl.BlockSpec((seq, SB_V_DIM), lambda b, h, *_: (b, v_col0 + h)),
                pl.BlockSpec((tk, tk), lambda b, h, *_: (0, 0)),
            ],
            out_specs=pl.BlockSpec((seq, SB_V_DIM), lambda b, h, *_: (b, h)),
            scratch_shapes=[stage_f32, stage_f32, stage_bf16, stage_bf16,
                            stage_f32, stage_f32, stage_f32, stage_f32,
                            stage_col, stage_col, stage_col, stage_col,
                            pltpu.VMEM((nq, tq, SB_V_DIM), F32), pltpu.VMEM((nq, tq, 1), F32)],
        ),
        out_shape=jax.ShapeDtypeStruct((n, vw), BF16),
        compiler_params=_params(("parallel", "parallel")),
        name="sb_attention",
    )(*tables, qg, kv, kv, qg, suffix_mat)


def kernel(x, ret_norm_pre, ret_w_in, ret_w_out, ret_norm_post, kv_norm, w_kv,
           sb_norm_pre, sb_w_in, sb_w_out, sb_norm_post):
    batch, seq, d = x.shape
    assert ret_w_in.shape[0] == 1 and sb_w_in.shape[0] == 1
    n = batch * seq
    x2d = x.reshape(n, d)

    pos = jnp.arange(seq, dtype=F32)
    half = RET_QK_DIM // 2
    inv_freq = 1.0 / (ROPE_BASE ** (jnp.arange(half, dtype=F32) / half))
    ang = pos[:, None] * inv_freq[None, :]
    cos, sin = jnp.cos(ang), jnp.sin(ang)

    vec = lambda g: g.reshape(1, d).astype(F32)

    proj = _in_proj(x2d, vec(ret_norm_pre[0]), ret_w_in[0].astype(BF16), cos, sin, seq=seq)
    o = _retention(proj, _retention_tables(), batch=batch, seq=seq)
    h1, u_kv, u_sb = _out_proj(o, ret_w_out[0].astype(BF16), vec(ret_norm_post[0]), x2d,
                               [vec(kv_norm), vec(sb_norm_pre[0])])

    kv = _matmul(u_kv, w_kv.astype(BF16), name="kv_proj")
    qg = _matmul(u_sb, sb_w_in[0].astype(BF16), name="sb_in_proj")
    o2 = _sb_attention(qg, kv, batch=batch, seq=seq)
    (out,) = _out_proj(o2, sb_w_out[0].astype(BF16), vec(sb_norm_post[0]), h1, [])
    return out.reshape(batch, seq, d)
```

```python
import functools
import math

import jax
import jax.numpy as jnp
from jax import lax
from jax.experimental import pallas as pl
from jax.experimental.pallas import tpu as pltpu

EPS = 1e-6
ROPE_BASE = 10000.0

RET_HEADS = 4
RET_QK_DIM = 256
RET_V_DIM = 512
RET_CHUNK = 256

SB_HEADS = 8
SB_QK_DIM = 128
SB_V_DIM = 256

VMEM_LIMIT_BYTES = 56 * 1024 * 1024

MASKED_LOG2_WEIGHT = -1e30

F32 = jnp.float32
BF16 = jnp.bfloat16


def _params(semantics):
    return pltpu.CompilerParams(dimension_semantics=semantics,
                                vmem_limit_bytes=VMEM_LIMIT_BYTES)


def _rms_scale(x):
    return x * lax.rsqrt(jnp.mean(x * x, axis=-1, keepdims=True) + EPS)


def _in_proj_kernel(x_ref, gain_ref, w_ref, cos_ref, sin_ref, o_ref, u_ref, *,
                    n_rope_groups, n_q_groups, k_scale, sub_n):
    j = pl.program_id(1)

    @pl.when(j == 0)
    def _():
        u_ref[...] = (_rms_scale(x_ref[...]) * gain_ref[...]).astype(BF16)

    tn = o_ref.shape[1]

    @pl.when(j == 0)
    def _():
        cos = cos_ref[...]
        sin = sin_ref[...]
        half = RET_QK_DIM // 2
        for grp in range(n_rope_groups):
            c0 = grp * RET_QK_DIM
            acc = jnp.dot(u_ref[...], w_ref[:, c0:c0 + RET_QK_DIM],
                          preferred_element_type=F32)
            if grp >= n_q_groups:
                acc = acc * k_scale
            x1 = acc[:, :half]
            x2 = acc[:, half:]
            o_ref[:, c0:c0 + half] = (x1 * cos - x2 * sin).astype(o_ref.dtype)
            o_ref[:, c0 + half:c0 + RET_QK_DIM] = (x1 * sin + x2 * cos).astype(o_ref.dtype)

    @pl.when(j != 0)
    def _():
        for c0 in range(0, tn, sub_n):
            o_ref[:, c0:c0 + sub_n] = jnp.dot(
                u_ref[...], w_ref[:, c0:c0 + sub_n],
                preferred_element_type=F32).astype(o_ref.dtype)


def _in_proj(x2d, gain, w, cos, sin, *, seq, tm=1024, tn=2048):
    n, d = x2d.shape
    n_out = w.shape[1]
    n_rope_groups = 2 * RET_HEADS
    assert n_rope_groups * RET_QK_DIM == tn and n % tm == 0 and seq % tm == 0 and n_out % tn == 0
    pos_blocks = seq // tm
    kern = functools.partial(_in_proj_kernel, n_rope_groups=n_rope_groups,
                             n_q_groups=RET_HEADS, k_scale=RET_QK_DIM ** -0.5, sub_n=512)
    return pl.pallas_call(
        kern,
        grid=(n // tm, n_out // tn),
        in_specs=[
            pl.BlockSpec((tm, d), lambda i, j: (i, 0)),
            pl.BlockSpec((1, d), lambda i, j: (0, 0)),
            pl.BlockSpec((d, tn), lambda i, j: (0, j)),
            pl.BlockSpec((tm, RET_QK_DIM // 2), lambda i, j: (i % pos_blocks, 0)),
            pl.BlockSpec((tm, RET_QK_DIM // 2), lambda i, j: (i % pos_blocks, 0)),
        ],
        out_specs=pl.BlockSpec((tm, tn), lambda i, j: (i, j)),
        out_shape=jax.ShapeDtypeStruct((n, n_out), BF16),
        scratch_shapes=[pltpu.VMEM((tm, d), BF16)],
        compiler_params=_params(("parallel", "arbitrary")),
        name="ret_in_proj",
    )(x2d, gain, w, cos, sin)


def _retention_kernel(q_ref, k_ref, v_ref, g_ref, dmask_ref, qdec_ref, kdec_ref, cdec_ref,
                      o_ref, state_ref, *, chunks_per_step):
    t = pl.program_id(1)

    @pl.when(t == 0)
    def _():
        state_ref[...] = jnp.zeros_like(state_ref)

    c = RET_CHUNK
    for ci in range(chunks_per_step):
        r0 = ci * c
        for h in range(RET_HEADS):
            q = q_ref[r0:r0 + c, h * RET_QK_DIM:(h + 1) * RET_QK_DIM]
            k = k_ref[r0:r0 + c, h * RET_QK_DIM:(h + 1) * RET_QK_DIM]
            v = v_ref[r0:r0 + c, h * RET_V_DIM:(h + 1) * RET_V_DIM]
            g = g_ref[r0:r0 + c, h * RET_V_DIM:(h + 1) * RET_V_DIM].astype(F32)
            state = state_ref[h]
            scores = lax.dot_general(q, k, (((1,), (1,)), ((), ())),
                                     preferred_element_type=F32) * dmask_ref[h]
            o = jnp.dot(scores.astype(BF16), v, preferred_element_type=F32)
            o = o + jnp.dot(q, state.astype(BF16), preferred_element_type=F32) * qdec_ref[h]
            v_dec = (v.astype(F32) * kdec_ref[h]).astype(BF16)
            state_ref[h] = state * cdec_ref[h] + lax.dot_general(
                k, v_dec, (((0,), (0,)), ((), ())), preferred_element_type=F32)
            mu = jnp.mean(o, axis=-1, keepdims=True)
            d = o - mu
            var = jnp.mean(d * d, axis=-1, keepdims=True)
            on = d * lax.rsqrt(var + EPS)
            gate = g / (1.0 + jnp.exp(-g))
            o_ref[r0:r0 + c, h * RET_V_DIM:(h + 1) * RET_V_DIM] = (on * gate).astype(o_ref.dtype)


def _retention(proj, tables, *, batch, seq, rows=512):
    n = proj.shape[0]
    qw = RET_HEADS * RET_QK_DIM
    vw = RET_HEADS * RET_V_DIM
    assert proj.shape[1] == 2 * qw + 2 * vw and vw == 2 * qw
    assert seq % rows == 0 and rows % RET_CHUNK == 0
    steps = seq // rows
    dmask, qdec, kdec, cdec = tables
    kern = functools.partial(_retention_kernel, chunks_per_step=rows // RET_CHUNK)
    row = lambda b, t: b * steps + t
    whole = lambda a: pl.BlockSpec(a.shape, lambda b, t: (0,) * a.ndim)
    return pl.pallas_call(
        kern,
        grid=(batch, steps),
        in_specs=[
            pl.BlockSpec((rows, qw), lambda b, t: (row(b, t), 0)),
            pl.BlockSpec((rows, qw), lambda b, t: (row(b, t), 1)),
            pl.BlockSpec((rows, vw), lambda b, t: (row(b, t), 1)),
            pl.BlockSpec((rows, vw), lambda b, t: (row(b, t), 2)),
            whole(dmask), whole(qdec), whole(kdec), whole(cdec),
        ],
        out_specs=pl.BlockSpec((rows, vw), lambda b, t: (row(b, t), 0)),
        out_shape=jax.ShapeDtypeStruct((n, vw), BF16),
        scratch_shapes=[pltpu.VMEM((RET_HEADS, RET_QK_DIM, RET_V_DIM), F32)],
        compiler_params=_params(("parallel", "arbitrary")),
        name="retention",
    )(proj, proj, proj, proj, dmask, qdec, kdec, cdec)


def _retention_tables():
    c = RET_CHUNK
    log_gamma = jnp.log1p(-jnp.exp2(-5.0 - jnp.arange(RET_HEADS, dtype=F32)))
    idx = jnp.arange(c, dtype=F32)
    diff = idx[:, None] - idx[None, :]
    dmask = jnp.where(diff >= 0, jnp.exp(log_gamma[:, None, None] * diff), 0.0)
    qdec = jnp.exp(log_gamma[:, None] * (idx[None, :] + 1.0))[:, :, None]
    kdec = jnp.exp(log_gamma[:, None] * (c - 1.0 - idx[None, :]))[:, :, None]
    cdec = jnp.broadcast_to(jnp.exp(log_gamma * c)[:, None, None], (RET_HEADS, 1, RET_V_DIM))
    return dmask, qdec, kdec, cdec


def _out_proj_kernel(o_ref, w_ref, gain_ref, res_ref, *rest, n_next):
    next_gain_refs = rest[:n_next]
    h_ref = rest[n_next]
    u_refs = rest[n_next + 1:]
    y = jnp.dot(o_ref[...], w_ref[...], preferred_element_type=F32)
    h = res_ref[...] + _rms_scale(y) * gain_ref[...]
    h_ref[...] = h
    if n_next:
        hn = _rms_scale(h)
        for gr, ur in zip(next_gain_refs, u_refs):
            ur[...] = (hn * gr[...]).astype(ur.dtype)


def _out_proj(o, w, gain, res, next_gains, *, tm=512):
    n, kdim = o.shape
    d = w.shape[1]
    n_next = len(next_gains)
    row_spec = lambda width: pl.BlockSpec((tm, width), lambda i: (i, 0))
    vec_spec = pl.BlockSpec((1, d), lambda i: (0, 0))
    outs = pl.pallas_call(
        functools.partial(_out_proj_kernel, n_next=n_next),
        grid=(n // tm,),
        in_specs=[row_spec(kdim), pl.BlockSpec((kdim, d), lambda i: (0, 0)), vec_spec,
                  row_spec(d)] + [vec_spec] * n_next,
        out_specs=[row_spec(d)] + [row_spec(d)] * n_next,
        out_shape=[jax.ShapeDtypeStruct((n, d), F32)]
        + [jax.ShapeDtypeStruct((n, d), BF16)] * n_next,
        compiler_params=_params(("parallel",)),
        name="out_proj_%d" % n_next,
    )(o, w, gain, res, *next_gains)
    return outs


def _matmul_kernel(a_ref, b_ref, *rest, sub_n):
    s_ref = rest[0] if len(rest) == 2 else None
    o_ref = rest[-1]
    for c0 in range(0, o_ref.shape[1], sub_n):
        acc = jnp.dot(a_ref[...], b_ref[:, c0:c0 + sub_n], preferred_element_type=F32)
        if s_ref is not None:
            acc = acc * s_ref[:, c0:c0 + sub_n]
        o_ref[:, c0:c0 + sub_n] = acc.astype(o_ref.dtype)


def _matmul(a, b, col_scale=None, *, name, tm=1024, tn=1536):
    n, kdim = a.shape
    n_out = b.shape[1]
    assert n % tm == 0 and n_out % tn == 0
    operands = [a, b]
    in_specs = [pl.BlockSpec((tm, kdim), lambda i, j: (i, 0)),
                pl.BlockSpec((kdim, tn), lambda i, j: (0, j))]
    if col_scale is not None:
        assert col_scale.shape == (1, n_out)
        operands.append(col_scale)
        in_specs.append(pl.BlockSpec((1, tn), lambda i, j: (0, j)))
    return pl.pallas_call(
        functools.partial(_matmul_kernel, sub_n=512),
        grid=(n // tm, n_out // tn),
        in_specs=in_specs,
        out_specs=pl.BlockSpec((tm, tn), lambda i, j: (i, j)),
        out_shape=jax.ShapeDtypeStruct((n, n_out), BF16),
        compiler_params=_params(("parallel", "arbitrary")),
        name=name,
    )(*operands)


def _sb_kernel(pq_ref, pj_ref, q_ref, k_ref, v_ref, g_ref, mm_ref, o_ref,
               za0_ref, za1_ref, zb0_ref, zb1_ref, pba0_ref, pba1_ref, pbb0_ref, pbb1_ref,
               zpa0_ref, zpa1_ref, zpb0_ref, zpb1_ref, p0a0_ref, p0a1_ref, p0b0_ref, p0b1_ref,
               wa_ref, wb_ref, acc_ref, carry_ref, *, tq, tk, n_masked, n_pairs):
    z_refs = ((za0_ref, za1_ref), (zb0_ref, zb1_ref))
    pb_refs = ((pba0_ref, pba1_ref), (pbb0_ref, pbb1_ref))
    p0_refs = ((p0a0_ref, p0a1_ref), (p0b0_ref, p0b1_ref))
    zp_refs = ((zpa0_ref, zpa1_ref), (zpb0_ref, zpb1_ref))
    w_refs = (wa_ref, wb_ref)

    def rows(qi):
        return pl.ds(pl.multiple_of(qi * tq, tq), tq)

    def qk(p, par):
        qi, jh = pq_ref[p], pj_ref[p]
        q = q_ref[rows(qi), :]
        for e in range(2):
            kb = k_ref[pl.ds(pl.multiple_of((jh - e) * tk, tk), tk), :]
            z_refs[par][e][...] = lax.dot_general(q, kb, (((1,), (1,)), ((), ())),
                                                  preferred_element_type=F32)

    def score(p, par, masked):
        qi, jh = pq_ref[p], pj_ref[p]
        for e in range(2):
            z2 = z_refs[par][e][...]
            t = jnp.log2(1.0 + jnp.exp2(-jnp.abs(z2)))
            pos = jnp.maximum(z2, 0.0) + t
            zp = z2 - pos
            if masked:
                col_minus_row = (lax.broadcasted_iota(jnp.int32, (tq, tk), 1)
                                 - lax.broadcasted_iota(jnp.int32, (tq, tk), 0))
                causal = col_minus_row < qi * tq - (jh - e) * tk
                pos = jnp.where(causal, pos, 0.0)
                zp = jnp.where(causal, zp, MASKED_LOG2_WEIGHT)
            pb_refs[par][e][...] = pos.astype(BF16)
            p0_refs[par][e][...] = pos[:, 0:1]
            zp_refs[par][e][...] = zp

    def weights(p, par, first):
        qi = pq_ref[p]
        carry = None if first else carry_ref[qi]
        for e in range(2):
            sfx = jnp.dot(pb_refs[par][e][...], mm_ref[...], preferred_element_type=F32)
            x = zp_refs[par][e][...] - sfx
            if carry is not None:
                x = x - carry
            w_refs[par][:, (1 - e) * tk:(2 - e) * tk] = jnp.exp2(x).astype(BF16)
            row_sum = sfx[:, 0:1] + p0_refs[par][e][...]
            carry = row_sum if carry is None else carry + row_sum
        carry_ref[qi] = carry

    def pv(p, par, first):
        qi, jh = pq_ref[p], pj_ref[p]
        vb = v_ref[pl.ds(pl.multiple_of((jh - 1) * tk, tk), 2 * tk), :]
        out = jnp.dot(w_refs[par][...], vb, preferred_element_type=F32)
        if first:
            acc_ref[qi] = out
        else:
            acc_ref[qi] += out

    def body(b, par, masked_of):
        if masked_of(2) is not None:
            weights(b - 2, par, first=masked_of(2))
        if masked_of(3) is not None:
            pv(b - 3, 1 - par, first=masked_of(3))
        if masked_of(0) is not None:
            qk(b, par)
        if masked_of(1) is not None:
            score(b - 1, 1 - par, masked=masked_of(1))

    def static_flags(b):
        def masked_of(s):
            p = b - s
            return None if not 0 <= p < n_pairs else p < n_masked
        return masked_of

    def peel(lo, hi):
        for b in range(lo, hi):
            body(b, b % 2, static_flags(b))

    def loop(lo, hi, flag):
        assert (hi - lo) % 2 == 0

        def step(m, c):
            for d in range(2):
                body(lo + 2 * m + d, (lo + d) % 2, lambda s: flag)
            return c
        lax.fori_loop(0, (hi - lo) // 2, step, 0)

    depth = 3
    m_lo = depth + (n_masked - depth) % 2
    u_lo = n_masked + depth + (n_pairs - n_masked - depth) % 2
    assert m_lo <= n_masked and u_lo <= n_pairs
    peel(0, m_lo)
    loop(m_lo, n_masked, True)
    peel(n_masked, u_lo)
    loop(u_lo, n_pairs, False)
    peel(n_pairs, n_pairs + depth)

    for qi in range(acc_ref.shape[0]):
        g = g_ref[qi * tq:(qi + 1) * tq, :].astype(F32)
        o_ref[qi * tq:(qi + 1) * tq, :] = (acc_ref[qi] * (g / (1.0 + jnp.exp(-g)))).astype(o_ref.dtype)


def _sb_attention(qg, kv, *, batch, seq, tq=512, tk=256):
    n = qg.shape[0]
    qkw = SB_HEADS * SB_QK_DIM
    vw = SB_HEADS * SB_V_DIM
    assert qg.shape[1] == qkw + vw and kv.shape[1] == qkw + vw and vw == 2 * qkw
    assert seq % tq == 0 and tq == 2 * tk
    nq = seq // tq
    v_col0 = qkw // SB_V_DIM
    rr = lax.broadcasted_iota(jnp.int32, (tk, tk), 0)
    cc = lax.broadcasted_iota(jnp.int32, (tk, tk), 1)
    suffix_mat = (rr > cc).astype(BF16)
    pairs = [(qi, 2 * qi + 1) for qi in range(nq)]
    pairs += [(qi, jh) for qi in range(nq) for jh in range(2 * qi - 1, 0, -2)]
    tables = [jnp.asarray(col, jnp.int32) for col in zip(*pairs)]
    kern = functools.partial(_sb_kernel, tq=tq, tk=tk, n_masked=nq, n_pairs=len(pairs))
    stage_f32 = pltpu.VMEM((tq, tk), F32)
    stage_bf16 = pltpu.VMEM((tq, tk), BF16)
    stage_col = pltpu.VMEM((tq, 1), F32)
    return pl.pallas_call(
        kern,
        grid_spec=pltpu.PrefetchScalarGridSpec(
            num_scalar_prefetch=len(tables),
            grid=(batch, SB_HEADS),
            in_specs=[
                pl.BlockSpec((seq, SB_QK_DIM), lambda b, h, *_: (b, h)),
                pl.BlockSpec((seq, SB_QK_DIM), lambda b, h, *_: (b, h)),
                pl.BlockSpec((seq, SB_V_DIM), lambda b, h, *_: (b, v_col0 + h)),
                pl.BlockSpec((seq, SB_V_DIM), lambda b, h, *_: (b, v_col0 + h)),
                pl.BlockSpec((tk, tk), lambda b, h, *_: (0, 0)),
            ],
            out_specs=pl.BlockSpec((seq, SB_V_DIM), lambda b, h, *_: (b, h)),
            scratch_shapes=[stage_f32] * 4 + [stage_bf16] * 4 + [stage_f32] * 4 + [stage_col] * 4
            + [pltpu.VMEM((tq, 2 * tk), BF16)] * 2 + [
                pltpu.VMEM((nq, tq, SB_V_DIM), F32), pltpu.VMEM((nq, tq, 1), F32)],
        ),
        out_shape=jax.ShapeDtypeStruct((n, vw), BF16),
        compiler_params=_params(("parallel", "parallel")),
        name="sb_attention",
    )(*tables, qg, kv, kv, qg, suffix_mat)


def kernel(x, ret_norm_pre, ret_w_in, ret_w_out, ret_norm_post, kv_norm, w_kv,
           sb_norm_pre, sb_w_in, sb_w_out, sb_norm_post):
    batch, seq, d = x.shape
    assert ret_w_in.shape[0] == 1 and sb_w_in.shape[0] == 1
    n = batch * seq
    x2d = x.reshape(n, d)

    pos = jnp.arange(seq, dtype=F32)
    half = RET_QK_DIM // 2
    inv_freq = 1.0 / (ROPE_BASE ** (jnp.arange(half, dtype=F32) / half))
    ang = pos[:, None] * inv_freq[None, :]
    cos, sin = jnp.cos(ang), jnp.sin(ang)

    vec = lambda g: g.reshape(1, d).astype(F32)

    proj = _in_proj(x2d, vec(ret_norm_pre[0]), ret_w_in[0].astype(BF16), cos, sin, seq=seq)
    o = _retention(proj, _retention_tables(), batch=batch, seq=seq)
    h1, u_kv, u_sb = _out_proj(o, ret_w_out[0].astype(BF16), vec(ret_norm_post[0]), x2d,
                               [vec(kv_norm), vec(sb_norm_pre[0])])

    qkw = SB_HEADS * SB_QK_DIM
    q_scale = math.log2(math.e) / math.sqrt(SB_QK_DIM)
    col_scale = jnp.ones((1, sb_w_in.shape[2]), F32).at[:, :qkw].set(q_scale)
    kv = _matmul(u_kv, w_kv.astype(BF16), name="kv_proj")
    qg = _matmul(u_sb, sb_w_in[0].astype(BF16), col_scale, name="sb_in_proj")
    o2 = _sb_attention(qg, kv, batch=batch, seq=seq)
    (out,) = _out_proj(o2, sb_w_out[0].astype(BF16), vec(sb_norm_post[0]), h1, [])
    return out.reshape(batch, seq, d)
```

```python
import functools
import math

import jax
import jax.numpy as jnp
from jax import lax
from jax.experimental import pallas as pl
from jax.experimental.pallas import tpu as pltpu

EPS = 1e-6
ROPE_BASE = 10000.0

RET_HEADS = 4
RET_QK_DIM = 256
RET_V_DIM = 512
RET_CHUNK = 256

SB_HEADS = 8
SB_QK_DIM = 128
SB_V_DIM = 256

VMEM_LIMIT_BYTES = 56 * 1024 * 1024

MASKED_LOG2_WEIGHT = -1e30

F32 = jnp.float32
BF16 = jnp.bfloat16


def _params(semantics):
    return pltpu.CompilerParams(dimension_semantics=semantics,
                                vmem_limit_bytes=VMEM_LIMIT_BYTES)


def _rms_scale(x):
    return x * lax.rsqrt(jnp.mean(x * x, axis=-1, keepdims=True) + EPS)


def _in_proj_kernel(x_ref, gain_ref, w_ref, cos_ref, sin_ref, o_ref, u_ref, *,
                    n_rope_groups, n_q_groups, k_scale, sub_n):
    j = pl.program_id(1)

    @pl.when(j == 0)
    def _():
        u_ref[...] = (_rms_scale(x_ref[...]) * gain_ref[...]).astype(BF16)

    tn = o_ref.shape[1]

    @pl.when(j == 0)
    def _():
        cos = cos_ref[...]
        sin = sin_ref[...]
        half = RET_QK_DIM // 2
        for grp in range(n_rope_groups):
            c0 = grp * RET_QK_DIM
            acc = jnp.dot(u_ref[...], w_ref[:, c0:c0 + RET_QK_DIM],
                          preferred_element_type=F32)
            if grp >= n_q_groups:
                acc = acc * k_scale
            x1 = acc[:, :half]
            x2 = acc[:, half:]
            o_ref[:, c0:c0 + half] = (x1 * cos - x2 * sin).astype(o_ref.dtype)
            o_ref[:, c0 + half:c0 + RET_QK_DIM] = (x1 * sin + x2 * cos).astype(o_ref.dtype)

    @pl.when(j != 0)
    def _():
        for c0 in range(0, tn, sub_n):
            o_ref[:, c0:c0 + sub_n] = jnp.dot(
                u_ref[...], w_ref[:, c0:c0 + sub_n],
                preferred_element_type=F32).astype(o_ref.dtype)


def _in_proj(x2d, gain, w, cos, sin, *, seq, tm=1024, tn=2048):
    n, d = x2d.shape
    n_out = w.shape[1]
    n_rope_groups = 2 * RET_HEADS
    assert n_rope_groups * RET_QK_DIM == tn and n % tm == 0 and seq % tm == 0 and n_out % tn == 0
    pos_blocks = seq // tm
    kern = functools.partial(_in_proj_kernel, n_rope_groups=n_rope_groups,
                             n_q_groups=RET_HEADS, k_scale=RET_QK_DIM ** -0.5, sub_n=512)
    return pl.pallas_call(
        kern,
        grid=(n // tm, n_out // tn),
        in_specs=[
            pl.BlockSpec((tm, d), lambda i, j: (i, 0)),
            pl.BlockSpec((1, d), lambda i, j: (0, 0)),
            pl.BlockSpec((d, tn), lambda i, j: (0, j)),
            pl.BlockSpec((tm, RET_QK_DIM // 2), lambda i, j: (i % pos_blocks, 0)),
            pl.BlockSpec((tm, RET_QK_DIM // 2), lambda i, j: (i % pos_blocks, 0)),
        ],
        out_specs=pl.BlockSpec((tm, tn), lambda i, j: (i, j)),
        out_shape=jax.ShapeDtypeStruct((n, n_out), BF16),
        scratch_shapes=[pltpu.VMEM((tm, d), BF16)],
        compiler_params=_params(("parallel", "arbitrary")),
        name="ret_in_proj",
    )(x2d, gain, w, cos, sin)


def _retention_kernel(q_ref, k_ref, v_ref, g_ref, dmask_ref, qdec_ref, kdec_ref, cdec_ref,
                      o_ref, state_ref, *, chunks_per_step):
    t = pl.program_id(1)

    @pl.when(t == 0)
    def _():
        state_ref[...] = jnp.zeros_like(state_ref)

    c = RET_CHUNK
    for ci in range(chunks_per_step):
        r0 = ci * c
        for h in range(RET_HEADS):
            q = q_ref[r0:r0 + c, h * RET_QK_DIM:(h + 1) * RET_QK_DIM]
            k = k_ref[r0:r0 + c, h * RET_QK_DIM:(h + 1) * RET_QK_DIM]
            v = v_ref[r0:r0 + c, h * RET_V_DIM:(h + 1) * RET_V_DIM]
            g = g_ref[r0:r0 + c, h * RET_V_DIM:(h + 1) * RET_V_DIM].astype(F32)
            state = state_ref[h]
            scores = lax.dot_general(q, k, (((1,), (1,)), ((), ())),
                                     preferred_element_type=F32) * dmask_ref[h]
            o = jnp.dot(scores.astype(BF16), v, preferred_element_type=F32)
            o = o + jnp.dot(q, state.astype(BF16), preferred_element_type=F32) * qdec_ref[h]
            v_dec = (v.astype(F32) * kdec_ref[h]).astype(BF16)
            state_ref[h] = state * cdec_ref[h] + lax.dot_general(
                k, v_dec, (((0,), (0,)), ((), ())), preferred_element_type=F32)
            mu = jnp.mean(o, axis=-1, keepdims=True)
            d = o - mu
            var = jnp.mean(d * d, axis=-1, keepdims=True)
            on = d * lax.rsqrt(var + EPS)
            gate = g / (1.0 + jnp.exp(-g))
            o_ref[r0:r0 + c, h * RET_V_DIM:(h + 1) * RET_V_DIM] = (on * gate).astype(o_ref.dtype)


def _retention(proj, tables, *, batch, seq, rows=512):
    n = proj.shape[0]
    qw = RET_HEADS * RET_QK_DIM
    vw = RET_HEADS * RET_V_DIM
    assert proj.shape[1] == 2 * qw + 2 * vw and vw == 2 * qw
    assert seq % rows == 0 and rows % RET_CHUNK == 0
    steps = seq // rows
    dmask, qdec, kdec, cdec = tables
    kern = functools.partial(_retention_kernel, chunks_per_step=rows // RET_CHUNK)
    row = lambda b, t: b * steps + t
    whole = lambda a: pl.BlockSpec(a.shape, lambda b, t: (0,) * a.ndim)
    return pl.pallas_call(
        kern,
        grid=(batch, steps),
        in_specs=[
            pl.BlockSpec((rows, qw), lambda b, t: (row(b, t), 0)),
            pl.BlockSpec((rows, qw), lambda b, t: (row(b, t), 1)),
            pl.BlockSpec((rows, vw), lambda b, t: (row(b, t), 1)),
            pl.BlockSpec((rows, vw), lambda b, t: (row(b, t), 2)),
            whole(dmask), whole(qdec), whole(kdec), whole(cdec),
        ],
        out_specs=pl.BlockSpec((rows, vw), lambda b, t: (row(b, t), 0)),
        out_shape=jax.ShapeDtypeStruct((n, vw), BF16),
        scratch_shapes=[pltpu.VMEM((RET_HEADS, RET_QK_DIM, RET_V_DIM), F32)],
        compiler_params=_params(("parallel", "arbitrary")),
        name="retention",
    )(proj, proj, proj, proj, dmask, qdec, kdec, cdec)


def _retention_tables():
    c = RET_CHUNK
    log_gamma = jnp.log1p(-jnp.exp2(-5.0 - jnp.arange(RET_HEADS, dtype=F32)))
    idx = jnp.arange(c, dtype=F32)
    diff = idx[:, None] - idx[None, :]
    dmask = jnp.where(diff >= 0, jnp.exp(log_gamma[:, None, None] * diff), 0.0)
    qdec = jnp.exp(log_gamma[:, None] * (idx[None, :] + 1.0))[:, :, None]
    kdec = jnp.exp(log_gamma[:, None] * (c - 1.0 - idx[None, :]))[:, :, None]
    cdec = jnp.broadcast_to(jnp.exp(log_gamma * c)[:, None, None], (RET_HEADS, 1, RET_V_DIM))
    return dmask, qdec, kdec, cdec


def _out_proj_kernel(o_ref, w_ref, gain_ref, res_ref, *rest, scaled, sub_m, sub_n):
    n_next = len(scaled)
    nexts = []
    pos = 0
    for has_scale in scaled:
        width = 3 if has_scale else 2
        nexts.append(rest[pos:pos + width])
        pos += width
    h_ref = rest[pos]
    out_refs = rest[pos + 1:]
    assert len(out_refs) == n_next
    for r0 in range(0, o_ref.shape[0], sub_m):
        rs = slice(r0, r0 + sub_m)
        y = jnp.dot(o_ref[rs, :], w_ref[...], preferred_element_type=F32)
        h = res_ref[rs, :] + _rms_scale(y) * gain_ref[...]
        h_ref[rs, :] = h
        if n_next:
            hn = _rms_scale(h)
        for refs, out_ref in zip(nexts, out_refs):
            u = (hn * refs[0][...]).astype(BF16)
            for c0 in range(0, out_ref.shape[1], sub_n):
                cs = slice(c0, c0 + sub_n)
                acc = jnp.dot(u, refs[1][:, cs], preferred_element_type=F32)
                if len(refs) == 3:
                    acc = acc * refs[2][:, cs]
                out_ref[rs, cs] = acc.astype(out_ref.dtype)


def _out_proj(o, w, gain, res, nexts, *, tm, sub_m, name, sub_n=512):
    n, kdim = o.shape
    d = w.shape[1]
    row_spec = lambda width: pl.BlockSpec((tm, width), lambda i: (i, 0))
    whole = lambda a: pl.BlockSpec(a.shape, lambda i: (0, 0), pipeline_mode=pl.Buffered(1))
    operands = [o, w, gain, res]
    in_specs = [row_spec(kdim), whole(w), whole(gain), row_spec(d)]
    scaled = []
    out_shape = [jax.ShapeDtypeStruct((n, d), F32)]
    out_specs = [row_spec(d)]
    for gain_i, w_i, scale_i in nexts:
        group = [gain_i, w_i] + ([] if scale_i is None else [scale_i])
        operands += group
        in_specs += [whole(a) for a in group]
        scaled.append(scale_i is not None)
        out_shape.append(jax.ShapeDtypeStruct((n, w_i.shape[1]), BF16))
        out_specs.append(row_spec(w_i.shape[1]))
    return pl.pallas_call(
        functools.partial(_out_proj_kernel, scaled=tuple(scaled), sub_m=sub_m, sub_n=sub_n),
        grid=(n // tm,),
        in_specs=in_specs,
        out_specs=out_specs,
        out_shape=out_shape,
        compiler_params=_params(("parallel",)),
        name=name,
    )(*operands)


def _sb_kernel(pq_ref, pj_ref, q_ref, k_ref, v_ref, g_ref, mm_ref, o_ref,
               za0_ref, za1_ref, zb0_ref, zb1_ref, pba0_ref, pba1_ref, pbb0_ref, pbb1_ref,
               zpa0_ref, zpa1_ref, zpb0_ref, zpb1_ref, p0a0_ref, p0a1_ref, p0b0_ref, p0b1_ref,
               wa_ref, wb_ref, acc_ref, carry_ref, *, tq, tk, n_masked, n_pairs):
    z_refs = ((za0_ref, za1_ref), (zb0_ref, zb1_ref))
    pb_refs = ((pba0_ref, pba1_ref), (pbb0_ref, pbb1_ref))
    p0_refs = ((p0a0_ref, p0a1_ref), (p0b0_ref, p0b1_ref))
    zp_refs = ((zpa0_ref, zpa1_ref), (zpb0_ref, zpb1_ref))
    w_refs = (wa_ref, wb_ref)

    def rows(qi):
        return pl.ds(pl.multiple_of(qi * tq, tq), tq)

    def qk(p, par):
        qi, jh = pq_ref[p], pj_ref[p]
        q = q_ref[rows(qi), :]
        for e in range(2):
            kb = k_ref[pl.ds(pl.multiple_of((jh - e) * tk, tk), tk), :]
            z_refs[par][e][...] = lax.dot_general(q, kb, (((1,), (1,)), ((), ())),
                                                  preferred_element_type=F32)

    def score(p, par, masked):
        qi, jh = pq_ref[p], pj_ref[p]
        for e in range(2):
            z2 = z_refs[par][e][...]
            t = jnp.log2(1.0 + jnp.exp2(-jnp.abs(z2)))
            pos = jnp.maximum(z2, 0.0) + t
            zp = z2 - pos
            if masked:
                col_minus_row = (lax.broadcasted_iota(jnp.int32, (tq, tk), 1)
                                 - lax.broadcasted_iota(jnp.int32, (tq, tk), 0))
                causal = col_minus_row < qi * tq - (jh - e) * tk
                pos = jnp.where(causal, pos, 0.0)
                zp = jnp.where(causal, zp, MASKED_LOG2_WEIGHT)
            pb_refs[par][e][...] = pos.astype(BF16)
            p0_refs[par][e][...] = pos[:, 0:1]
            zp_refs[par][e][...] = zp

    def weights(p, par, first):
        qi = pq_ref[p]
        carry = None if first else carry_ref[qi]
        for e in range(2):
            sfx = jnp.dot(pb_refs[par][e][...], mm_ref[...], preferred_element_type=F32)
            x = zp_refs[par][e][...] - sfx
            if carry is not None:
                x = x - carry
            w_refs[par][:, (1 - e) * tk:(2 - e) * tk] = jnp.exp2(x).astype(BF16)
            row_sum = sfx[:, 0:1] + p0_refs[par][e][...]
            carry = row_sum if carry is None else carry + row_sum
        carry_ref[qi] = carry

    def pv(p, par, first):
        qi, jh = pq_ref[p], pj_ref[p]
        vb = v_ref[pl.ds(pl.multiple_of((jh - 1) * tk, tk), 2 * tk), :]
        out = jnp.dot(w_refs[par][...], vb, preferred_element_type=F32)
        if first:
            acc_ref[qi] = out
        else:
            acc_ref[qi] += out

    def body(b, par, masked_of):
        if masked_of(2) is not None:
            weights(b - 2, par, first=masked_of(2))
        if masked_of(3) is not None:
            pv(b - 3, 1 - par, first=masked_of(3))
        if masked_of(0) is not None:
            qk(b, par)
        if masked_of(1) is not None:
            score(b - 1, 1 - par, masked=masked_of(1))

    def static_flags(b):
        def masked_of(s):
            p = b - s
            return None if not 0 <= p < n_pairs else p < n_masked
        return masked_of

    def peel(lo, hi):
        for b in range(lo, hi):
            body(b, b % 2, static_flags(b))

    def loop(lo, hi, flag, unroll):
        assert (hi - lo) % unroll == 0 and unroll % 2 == 0

        def step(m, c):
            for d in range(unroll):
                body(lo + unroll * m + d, (lo + d) % 2, lambda s: flag)
            return c
        lax.fori_loop(0, (hi - lo) // unroll, step, 0)

    depth = 3
    m_unroll, u_unroll = 2, 4
    m_lo = depth + (n_masked - depth) % m_unroll
    u_lo = n_masked + depth + (n_pairs - n_masked - depth) % u_unroll
    assert m_lo <= n_masked and u_lo <= n_pairs
    peel(0, m_lo)
    loop(m_lo, n_masked, True, m_unroll)
    peel(n_masked, u_lo)
    loop(u_lo, n_pairs, False, u_unroll)
    peel(n_pairs, n_pairs + depth)

    for qi in range(acc_ref.shape[0]):
        g = g_ref[qi * tq:(qi + 1) * tq, :].astype(F32)
        o_ref[qi * tq:(qi + 1) * tq, :] = (acc_ref[qi] * (g / (1.0 + jnp.exp(-g)))).astype(o_ref.dtype)


def _sb_attention(qg, kv, *, batch, seq, tq=512, tk=256):
    n = qg.shape[0]
    qkw = SB_HEADS * SB_QK_DIM
    vw = SB_HEADS * SB_V_DIM
    assert qg.shape[1] == qkw + vw and kv.shape[1] == qkw + vw and vw == 2 * qkw
    assert seq % tq == 0 and tq == 2 * tk
    nq = seq // tq
    v_col0 = qkw // SB_V_DIM
    rr = lax.broadcasted_iota(jnp.int32, (tk, tk), 0)
    cc = lax.broadcasted_iota(jnp.int32, (tk, tk), 1)
    suffix_mat = (rr > cc).astype(BF16)
    pairs = [(qi, 2 * qi + 1) for qi in range(nq)]
    pairs += [(qi, jh) for qi in range(nq) for jh in range(2 * qi - 1, 0, -2)]
    tables = [jnp.asarray(col, jnp.int32) for col in zip(*pairs)]
    kern = functools.partial(_sb_kernel, tq=tq, tk=tk, n_masked=nq, n_pairs=len(pairs))
    stage_f32 = pltpu.VMEM((tq, tk), F32)
    stage_bf16 = pltpu.VMEM((tq, tk), BF16)
    stage_col = pltpu.VMEM((tq, 1), F32)
    return pl.pallas_call(
        kern,
        grid_spec=pltpu.PrefetchScalarGridSpec(
            num_scalar_prefetch=len(tables),
            grid=(batch, SB_HEADS),
            in_specs=[
                pl.BlockSpec((seq, SB_QK_DIM), lambda b, h, *_: (b, h)),
                pl.BlockSpec((seq, SB_QK_DIM), lambda b, h, *_: (b, h)),
                pl.BlockSpec((seq, SB_V_DIM), lambda b, h, *_: (b, v_col0 + h)),
                pl.BlockSpec((seq, SB_V_DIM), lambda b, h, *_: (b, v_col0 + h)),
                pl.BlockSpec((tk, tk), lambda b, h, *_: (0, 0)),
            ],
            out_specs=pl.BlockSpec((seq, SB_V_DIM), lambda b, h, *_: (b, h)),
            scratch_shapes=[stage_f32] * 4 + [stage_bf16] * 4 + [stage_f32] * 4 + [stage_col] * 4
            + [pltpu.VMEM((tq, 2 * tk), BF16)] * 2 + [
                pltpu.VMEM((nq, tq, SB_V_DIM), F32), pltpu.VMEM((nq, tq, 1), F32)],
        ),
        out_shape=jax.ShapeDtypeStruct((n, vw), BF16),
        compiler_params=_params(("parallel", "parallel")),
        name="sb_attention",
    )(*tables, qg, kv, kv, qg, suffix_mat)


def kernel(x, ret_norm_pre, ret_w_in, ret_w_out, ret_norm_post, kv_norm, w_kv,
           sb_norm_pre, sb_w_in, sb_w_out, sb_norm_post):
    batch, seq, d = x.shape
    assert ret_w_in.shape[0] == 1 and sb_w_in.shape[0] == 1
    n = batch * seq
    x2d = x.reshape(n, d)

    pos = jnp.arange(seq, dtype=F32)
    half = RET_QK_DIM // 2
    inv_freq = 1.0 / (ROPE_BASE ** (jnp.arange(half, dtype=F32) / half))
    ang = pos[:, None] * inv_freq[None, :]
    cos, sin = jnp.cos(ang), jnp.sin(ang)

    vec = lambda g: g.reshape(1, d).astype(F32)

    proj = _in_proj(x2d, vec(ret_norm_pre[0]), ret_w_in[0].astype(BF16), cos, sin, seq=seq)
    o = _retention(proj, _retention_tables(), batch=batch, seq=seq)
    qkw = SB_HEADS * SB_QK_DIM
    q_scale = math.log2(math.e) / math.sqrt(SB_QK_DIM)
    col_scale = jnp.ones((1, sb_w_in.shape[2]), F32).at[:, :qkw].set(q_scale)
    h1, kv, qg = _out_proj(o, ret_w_out[0].astype(BF16), vec(ret_norm_post[0]), x2d,
                           [(vec(kv_norm), w_kv.astype(BF16), None),
                            (vec(sb_norm_pre[0]), sb_w_in[0].astype(BF16), col_scale)],
                           tm=512, sub_m=256, name="ret_out_kv_q_proj")
    o2 = _sb_attention(qg, kv, batch=batch, seq=seq)
    (out,) = _out_proj(o2, sb_w_out[0].astype(BF16), vec(sb_norm_post[0]), h1, [],
                       tm=1024, sub_m=256, name="sb_out_proj")
    return out.reshape(batch, seq, d)
```

```python
import functools
import math

import jax
import jax.numpy as jnp
from jax import lax
from jax.experimental import pallas as pl
from jax.experimental.pallas import tpu as pltpu

EPS = 1e-6
ROPE_BASE = 10000.0

RET_HEADS = 4
RET_QK_DIM = 256
RET_V_DIM = 512
RET_CHUNK = 256

SB_HEADS = 8
SB_QK_DIM = 128
SB_V_DIM = 256

VMEM_LIMIT_BYTES = 56 * 1024 * 1024

MASKED_LOG2_WEIGHT = -1e30

F32 = jnp.float32
BF16 = jnp.bfloat16


def _params(semantics):
    return pltpu.CompilerParams(dimension_semantics=semantics,
                                vmem_limit_bytes=VMEM_LIMIT_BYTES)


def _rms_scale(x):
    return x * lax.rsqrt(jnp.mean(x * x, axis=-1, keepdims=True) + EPS)


def _silu(g):
    return g / (1.0 + jnp.exp2(g * (-math.log2(math.e))))


def _in_proj_kernel(x_ref, gain_ref, w_ref, cos_ref, sin_ref, o_ref, *,
                    n_rope_groups, n_q_groups, k_scale, sub_m, sub_n):
    half = RET_QK_DIM // 2
    rope_cols = n_rope_groups * RET_QK_DIM
    for r0 in range(0, o_ref.shape[0], sub_m):
        rs = slice(r0, r0 + sub_m)
        u = (_rms_scale(x_ref[rs, :]) * gain_ref[...]).astype(BF16)
        cos = cos_ref[rs, :]
        sin = sin_ref[rs, :]
        for grp in range(n_rope_groups):
            c0 = grp * RET_QK_DIM
            acc = jnp.dot(u, w_ref[:, c0:c0 + RET_QK_DIM], preferred_element_type=F32)
            if grp >= n_q_groups:
                acc = acc * k_scale
            x1 = acc[:, :half]
            x2 = acc[:, half:]
            o_ref[rs, c0:c0 + half] = (x1 * cos - x2 * sin).astype(o_ref.dtype)
            o_ref[rs, c0 + half:c0 + RET_QK_DIM] = (x1 * sin + x2 * cos).astype(o_ref.dtype)
        for c0 in range(rope_cols, o_ref.shape[1], sub_n):
            o_ref[rs, c0:c0 + sub_n] = jnp.dot(
                u, w_ref[:, c0:c0 + sub_n], preferred_element_type=F32).astype(o_ref.dtype)


def _in_proj(x2d, gain, w, cos, sin, *, seq, tm=512, sub_m=256):
    n, d = x2d.shape
    n_out = w.shape[1]
    n_rope_groups = 2 * RET_HEADS
    assert n % tm == 0 and seq % tm == 0 and (n_out - n_rope_groups * RET_QK_DIM) % 512 == 0
    pos_blocks = seq // tm
    kern = functools.partial(_in_proj_kernel, n_rope_groups=n_rope_groups, n_q_groups=RET_HEADS,
                             k_scale=RET_QK_DIM ** -0.5, sub_m=sub_m, sub_n=512)
    whole = lambda a: pl.BlockSpec(a.shape, lambda i: (0, 0), pipeline_mode=pl.Buffered(1))
    return pl.pallas_call(
        kern,
        grid=(n // tm,),
        in_specs=[
            pl.BlockSpec((tm, d), lambda i: (i, 0)),
            whole(gain),
            whole(w),
            pl.BlockSpec((tm, RET_QK_DIM // 2), lambda i: (i % pos_blocks, 0)),
            pl.BlockSpec((tm, RET_QK_DIM // 2), lambda i: (i % pos_blocks, 0)),
        ],
        out_specs=pl.BlockSpec((tm, n_out), lambda i: (i, 0)),
        out_shape=jax.ShapeDtypeStruct((n, n_out), BF16),
        compiler_params=_params(("parallel",)),
        name="ret_in_proj",
    )(x2d, gain, w, cos, sin)


def _retention_kernel(q_ref, k_ref, v_ref, g_ref, dmask_ref, qdec_ref, kdec_ref, cdec_ref,
                      o_ref, state_ref, *, chunks_per_step):
    t = pl.program_id(1)

    @pl.when(t == 0)
    def _():
        state_ref[...] = jnp.zeros_like(state_ref)

    c = RET_CHUNK
    for ci in range(chunks_per_step):
        r0 = ci * c
        for h in range(RET_HEADS):
            q = q_ref[r0:r0 + c, h * RET_QK_DIM:(h + 1) * RET_QK_DIM]
            k = k_ref[r0:r0 + c, h * RET_QK_DIM:(h + 1) * RET_QK_DIM]
            v = v_ref[r0:r0 + c, h * RET_V_DIM:(h + 1) * RET_V_DIM]
            g = g_ref[r0:r0 + c, h * RET_V_DIM:(h + 1) * RET_V_DIM].astype(F32)
            state = state_ref[h]
            scores = lax.dot_general(q, k, (((1,), (1,)), ((), ())),
                                     preferred_element_type=F32) * dmask_ref[h]
            q_dec = (q.astype(F32) * qdec_ref[h]).astype(BF16)
            o = jnp.dot(jnp.concatenate([scores.astype(BF16), q_dec], axis=1),
                        jnp.concatenate([v, state.astype(BF16)], axis=0),
                        preferred_element_type=F32)
            k_dec = (k.astype(F32) * kdec_ref[h]).astype(BF16)
            state_ref[h] = state * cdec_ref[h] + lax.dot_general(
                k_dec, v, (((0,), (0,)), ((), ())), preferred_element_type=F32)
            mu = jnp.mean(o, axis=-1, keepdims=True)
            d = o - mu
            var = jnp.mean(d * d, axis=-1, keepdims=True)
            on = d * lax.rsqrt(var + EPS)
            o_ref[r0:r0 + c, h * RET_V_DIM:(h + 1) * RET_V_DIM] = (on * _silu(g)).astype(o_ref.dtype)


def _retention(proj, tables, *, batch, seq, rows=512):
    n = proj.shape[0]
    qw = RET_HEADS * RET_QK_DIM
    vw = RET_HEADS * RET_V_DIM
    assert proj.shape[1] == 2 * qw + 2 * vw and vw == 2 * qw
    assert seq % rows == 0 and rows % RET_CHUNK == 0
    steps = seq // rows
    dmask, qdec, kdec, cdec = tables
    kern = functools.partial(_retention_kernel, chunks_per_step=rows // RET_CHUNK)
    row = lambda b, t: b * steps + t
    whole = lambda a: pl.BlockSpec(a.shape, lambda b, t: (0,) * a.ndim)
    return pl.pallas_call(
        kern,
        grid=(batch, steps),
        in_specs=[
            pl.BlockSpec((rows, qw), lambda b, t: (row(b, t), 0)),
            pl.BlockSpec((rows, qw), lambda b, t: (row(b, t), 1)),
            pl.BlockSpec((rows, vw), lambda b, t: (row(b, t), 1)),
            pl.BlockSpec((rows, vw), lambda b, t: (row(b, t), 2)),
            whole(dmask), whole(qdec), whole(kdec), whole(cdec),
        ],
        out_specs=pl.BlockSpec((rows, vw), lambda b, t: (row(b, t), 0)),
        out_shape=jax.ShapeDtypeStruct((n, vw), BF16),
        scratch_shapes=[pltpu.VMEM((RET_HEADS, RET_QK_DIM, RET_V_DIM), F32)],
        compiler_params=_params(("parallel", "arbitrary")),
        name="retention",
    )(proj, proj, proj, proj, dmask, qdec, kdec, cdec)


def _retention_tables():
    c = RET_CHUNK
    log_gamma = jnp.log1p(-jnp.exp2(-5.0 - jnp.arange(RET_HEADS, dtype=F32)))
    idx = jnp.arange(c, dtype=F32)
    diff = idx[:, None] - idx[None, :]
    dmask = jnp.where(diff >= 0, jnp.exp(log_gamma[:, None, None] * diff), 0.0)
    qdec = jnp.exp(log_gamma[:, None] * (idx[None, :] + 1.0))[:, :, None]
    kdec = jnp.exp(log_gamma[:, None] * (c - 1.0 - idx[None, :]))[:, :, None]
    cdec = jnp.broadcast_to(jnp.exp(log_gamma * c)[:, None, None], (RET_HEADS, 1, RET_V_DIM))
    return dmask, qdec, kdec, cdec


def _out_proj_kernel(o_ref, w_ref, gain_ref, res_ref, *rest, scaled, sub_m, sub_n):
    n_next = len(scaled)
    nexts = []
    pos = 0
    for has_scale in scaled:
        width = 3 if has_scale else 2
        nexts.append(rest[pos:pos + width])
        pos += width
    h_ref = rest[pos]
    out_refs = rest[pos + 1:]
    assert len(out_refs) == n_next
    for r0 in range(0, o_ref.shape[0], sub_m):
        rs = slice(r0, r0 + sub_m)
        y = jnp.dot(o_ref[rs, :], w_ref[...], preferred_element_type=F32)
        h = res_ref[rs, :] + _rms_scale(y) * gain_ref[...]
        h_ref[rs, :] = h
        if n_next:
            hn = _rms_scale(h)
        for refs, out_ref in zip(nexts, out_refs):
            u = (hn * refs[0][...]).astype(BF16)
            for c0 in range(0, out_ref.shape[1], sub_n):
                cs = slice(c0, c0 + sub_n)
                acc = jnp.dot(u, refs[1][:, cs], preferred_element_type=F32)
                if len(refs) == 3:
                    acc = acc * refs[2][:, cs]
                out_ref[rs, cs] = acc.astype(out_ref.dtype)


def _out_proj(o, w, gain, res, nexts, *, tm, sub_m, name, sub_n=512):
    n, kdim = o.shape
    d = w.shape[1]
    row_spec = lambda width: pl.BlockSpec((tm, width), lambda i: (i, 0))
    whole = lambda a: pl.BlockSpec(a.shape, lambda i: (0, 0), pipeline_mode=pl.Buffered(1))
    operands = [o, w, gain, res]
    in_specs = [row_spec(kdim), whole(w), whole(gain), row_spec(d)]
    scaled = []
    out_shape = [jax.ShapeDtypeStruct((n, d), F32)]
    out_specs = [row_spec(d)]
    for gain_i, w_i, scale_i in nexts:
        group = [gain_i, w_i] + ([] if scale_i is None else [scale_i])
        operands += group
        in_specs += [whole(a) for a in group]
        scaled.append(scale_i is not None)
        out_shape.append(jax.ShapeDtypeStruct((n, w_i.shape[1]), BF16))
        out_specs.append(row_spec(w_i.shape[1]))
    return pl.pallas_call(
        functools.partial(_out_proj_kernel, scaled=tuple(scaled), sub_m=sub_m, sub_n=sub_n),
        grid=(n // tm,),
        in_specs=in_specs,
        out_specs=out_specs,
        out_shape=out_shape,
        compiler_params=_params(("parallel",)),
        name=name,
    )(*operands)


def _sb_kernel(pq_ref, pj_ref, q_ref, k_ref, v_ref, g_ref, mm_ref, o_ref,
               za0_ref, za1_ref, zb0_ref, zb1_ref, pba0_ref, pba1_ref, pbb0_ref, pbb1_ref,
               zpa0_ref, zpa1_ref, zpb0_ref, zpb1_ref, p0a0_ref, p0a1_ref, p0b0_ref, p0b1_ref,
               wa_ref, wb_ref, acc_ref, carry_ref, *, tq, tk, n_masked, n_pairs):
    z_refs = ((za0_ref, za1_ref), (zb0_ref, zb1_ref))
    pb_refs = ((pba0_ref, pba1_ref), (pbb0_ref, pbb1_ref))
    p0_refs = ((p0a0_ref, p0a1_ref), (p0b0_ref, p0b1_ref))
    zp_refs = ((zpa0_ref, zpa1_ref), (zpb0_ref, zpb1_ref))
    w_refs = (wa_ref, wb_ref)

    def rows(qi):
        return pl.ds(pl.multiple_of(qi * tq, tq), tq)

    def qk(p, par):
        qi, jh = pq_ref[p], pj_ref[p]
        q = q_ref[rows(qi), :]
        for e in range(2):
            kb = k_ref[pl.ds(pl.multiple_of((jh - e) * tk, tk), tk), :]
            z_refs[par][e][...] = lax.dot_general(q, kb, (((1,), (1,)), ((), ())),
                                                  preferred_element_type=F32)

    def score(p, par, masked):
        qi, jh = pq_ref[p], pj_ref[p]
        for e in range(2):
            z2 = z_refs[par][e][...]
            t = jnp.log2(1.0 + jnp.exp2(-jnp.abs(z2)))
            pos = jnp.maximum(z2, 0.0) + t
            zp = z2 - pos
            if masked:
                col_minus_row = (lax.broadcasted_iota(jnp.int32, (tq, tk), 1)
                                 - lax.broadcasted_iota(jnp.int32, (tq, tk), 0))
                causal = col_minus_row < qi * tq - (jh - e) * tk
                pos = jnp.where(causal, pos, 0.0)
                zp = jnp.where(causal, zp, MASKED_LOG2_WEIGHT)
            pb_refs[par][e][...] = pos.astype(BF16)
            p0_refs[par][e][...] = pos[:, 0:1]
            zp_refs[par][e][...] = zp

    def weights(p, par, first):
        qi = pq_ref[p]
        carry = None if first else carry_ref[qi]
        for e in range(2):
            sfx = jnp.dot(pb_refs[par][e][...], mm_ref[...], preferred_element_type=F32)
            x = zp_refs[par][e][...] - sfx
            if carry is not None:
                x = x - carry
            w_refs[par][:, (1 - e) * tk:(2 - e) * tk] = jnp.exp2(x).astype(BF16)
            row_sum = sfx[:, 0:1] + p0_refs[par][e][...]
            carry = row_sum if carry is None else carry + row_sum
        carry_ref[qi] = carry

    def pv(p, par, first):
        qi, jh = pq_ref[p], pj_ref[p]
        vb = v_ref[pl.ds(pl.multiple_of((jh - 1) * tk, tk), 2 * tk), :]
        out = jnp.dot(w_refs[par][...], vb, preferred_element_type=F32)
        if first:
            acc_ref[qi] = out
        else:
            acc_ref[qi] += out

    def body(b, par, masked_of):
        if masked_of(2) is not None:
            weights(b - 2, par, first=masked_of(2))
        if masked_of(3) is not None:
            pv(b - 3, 1 - par, first=masked_of(3))
        if masked_of(0) is not None:
            qk(b, par)
        if masked_of(1) is not None:
            score(b - 1, 1 - par, masked=masked_of(1))

    def static_flags(b):
        def masked_of(s):
            p = b - s
            return None if not 0 <= p < n_pairs else p < n_masked
        return masked_of

    def peel(lo, hi):
        for b in range(lo, hi):
            body(b, b % 2, static_flags(b))

    def loop(lo, hi, flag, unroll):
        assert (hi - lo) % unroll == 0 and unroll % 2 == 0

        def step(m, c):
            for d in range(unroll):
                body(lo + unroll * m + d, (lo + d) % 2, lambda s: flag)
            return c
        lax.fori_loop(0, (hi - lo) // unroll, step, 0)

    depth = 3
    m_unroll, u_unroll = 2, 4
    m_lo = depth + (n_masked - depth) % m_unroll
    u_lo = n_masked + depth + (n_pairs - n_masked - depth) % u_unroll
    assert m_lo <= n_masked and u_lo <= n_pairs
    peel(0, m_lo)
    loop(m_lo, n_masked, True, m_unroll)
    peel(n_masked, u_lo)
    loop(u_lo, n_pairs, False, u_unroll)
    peel(n_pairs, n_pairs + depth)

    for qi in range(acc_ref.shape[0]):
        g = g_ref[qi * tq:(qi + 1) * tq, :].astype(F32)
        o_ref[qi * tq:(qi + 1) * tq, :] = (acc_ref[qi] * _silu(g)).astype(o_ref.dtype)


def _sb_attention(qg, kv, *, batch, seq, tq=512, tk=256):
    n = qg.shape[0]
    qkw = SB_HEADS * SB_QK_DIM
    vw = SB_HEADS * SB_V_DIM
    assert qg.shape[1] == qkw + vw and kv.shape[1] == qkw + vw and vw == 2 * qkw
    assert seq % tq == 0 and tq == 2 * tk
    nq = seq // tq
    v_col0 = qkw // SB_V_DIM
    rr = lax.broadcasted_iota(jnp.int32, (tk, tk), 0)
    cc = lax.broadcasted_iota(jnp.int32, (tk, tk), 1)
    suffix_mat = (rr > cc).astype(BF16)
    pairs = [(qi, 2 * qi + 1) for qi in range(nq)]
    pairs += [(qi, jh) for qi in range(nq) for jh in range(2 * qi - 1, 0, -2)]
    tables = [jnp.asarray(col, jnp.int32) for col in zip(*pairs)]
    kern = functools.partial(_sb_kernel, tq=tq, tk=tk, n_masked=nq, n_pairs=len(pairs))
    stage_f32 = pltpu.VMEM((tq, tk), F32)
    stage_bf16 = pltpu.VMEM((tq, tk), BF16)
    stage_col = pltpu.VMEM((tq, 1), F32)
    return pl.pallas_call(
        kern,
        grid_spec=pltpu.PrefetchScalarGridSpec(
            num_scalar_prefetch=len(tables),
            grid=(batch, SB_HEADS),
            in_specs=[
                pl.BlockSpec((seq, SB_QK_DIM), lambda b, h, *_: (b, h)),
                pl.BlockSpec((seq, SB_QK_DIM), lambda b, h, *_: (b, h)),
                pl.BlockSpec((seq, SB_V_DIM), lambda b, h, *_: (b, v_col0 + h)),
                pl.BlockSpec((seq, SB_V_DIM), lambda b, h, *_: (b, v_col0 + h)),
                pl.BlockSpec((tk, tk), lambda b, h, *_: (0, 0)),
            ],
            out_specs=pl.BlockSpec((seq, SB_V_DIM), lambda b, h, *_: (b, h)),
            scratch_shapes=[stage_f32] * 4 + [stage_bf16] * 4 + [stage_f32] * 4 + [stage_col] * 4
            + [pltpu.VMEM((tq, 2 * tk), BF16)] * 2 + [
                pltpu.VMEM((nq, tq, SB_V_DIM), F32), pltpu.VMEM((nq, tq, 1), F32)],
        ),
        out_shape=jax.ShapeDtypeStruct((n, vw), BF16),
        compiler_params=_params(("parallel", "parallel")),
        name="sb_attention",
    )(*tables, qg, kv, kv, qg, suffix_mat)


def kernel(x, ret_norm_pre, ret_w_in, ret_w_out, ret_norm_post, kv_norm, w_kv,
           sb_norm_pre, sb_w_in, sb_w_out, sb_norm_post):
    batch, seq, d = x.shape
    assert ret_w_in.shape[0] == 1 and sb_w_in.shape[0] == 1
    n = batch * seq
    x2d = x.reshape(n, d)

    pos = jnp.arange(seq, dtype=F32)
    half = RET_QK_DIM // 2
    inv_freq = 1.0 / (ROPE_BASE ** (jnp.arange(half, dtype=F32) / half))
    ang = pos[:, None] * inv_freq[None, :]
    cos, sin = jnp.cos(ang), jnp.sin(ang)

    vec = lambda g: g.reshape(1, d).astype(F32)

    proj = _in_proj(x2d, vec(ret_norm_pre[0]), ret_w_in[0].astype(BF16), cos, sin, seq=seq)
    o = _retention(proj, _retention_tables(), batch=batch, seq=seq)
    qkw = SB_HEADS * SB_QK_DIM
    q_scale = math.log2(math.e) / math.sqrt(SB_QK_DIM)
    col_scale = jnp.ones((1, sb_w_in.shape[2]), F32).at[:, :qkw].set(q_scale)
    h1, kv, qg = _out_proj(o, ret_w_out[0].astype(BF16), vec(ret_norm_post[0]), x2d,
                           [(vec(kv_norm), w_kv.astype(BF16), None),
                            (vec(sb_norm_pre[0]), sb_w_in[0].astype(BF16), col_scale)],
                           tm=512, sub_m=256, name="ret_out_kv_q_proj")
    o2 = _sb_attention(qg, kv, batch=batch, seq=seq)
    (out,) = _out_proj(o2, sb_w_out[0].astype(BF16), vec(sb_norm_post[0]), h1, [],
                       tm=1024, sub_m=256, name="sb_out_proj")
    return out.reshape(batch, seq, d)
```

```python
import functools
import math

import jax
import jax.numpy as jnp
from jax import lax
from jax.experimental import pallas as pl
from jax.experimental.pallas import tpu as pltpu

EPS = 1e-6
ROPE_BASE = 10000.0

RET_HEADS = 4
RET_QK_DIM = 256
RET_V_DIM = 512
RET_CHUNK = 256

SB_HEADS = 8
SB_QK_DIM = 128
SB_V_DIM = 256

VMEM_LIMIT_BYTES = 56 * 1024 * 1024

MASKED_LOG2_WEIGHT = -1e30

F32 = jnp.float32
BF16 = jnp.bfloat16


def _params(semantics):
    return pltpu.CompilerParams(dimension_semantics=semantics,
                                vmem_limit_bytes=VMEM_LIMIT_BYTES)


def _rms_scale(x):
    return x * lax.rsqrt(jnp.mean(x * x, axis=-1, keepdims=True) + EPS)


def _silu(g):
    return g / (1.0 + jnp.exp2(g * (-math.log2(math.e))))


def _in_proj_kernel(x_ref, gain_ref, w_ref, cos_ref, sin_ref, o_ref, *,
                    n_rope_groups, n_q_groups, k_scale, sub_m, sub_n):
    half = RET_QK_DIM // 2
    rope_cols = n_rope_groups * RET_QK_DIM
    for r0 in range(0, o_ref.shape[0], sub_m):
        rs = slice(r0, r0 + sub_m)
        u = (_rms_scale(x_ref[rs, :]) * gain_ref[...]).astype(BF16)
        cos = cos_ref[rs, :]
        sin = sin_ref[rs, :]
        for grp in range(n_rope_groups):
            c0 = grp * RET_QK_DIM
            acc = jnp.dot(u, w_ref[:, c0:c0 + RET_QK_DIM], preferred_element_type=F32)
            if grp >= n_q_groups:
                acc = acc * k_scale
            x1 = acc[:, :half]
            x2 = acc[:, half:]
            o_ref[rs, c0:c0 + half] = (x1 * cos - x2 * sin).astype(o_ref.dtype)
            o_ref[rs, c0 + half:c0 + RET_QK_DIM] = (x1 * sin + x2 * cos).astype(o_ref.dtype)
        for c0 in range(rope_cols, o_ref.shape[1], sub_n):
            o_ref[rs, c0:c0 + sub_n] = jnp.dot(
                u, w_ref[:, c0:c0 + sub_n], preferred_element_type=F32).astype(o_ref.dtype)


def _in_proj(x2d, gain, w, cos, sin, *, seq, tm=512, sub_m=256):
    n, d = x2d.shape
    n_out = w.shape[1]
    n_rope_groups = 2 * RET_HEADS
    assert n % tm == 0 and seq % tm == 0 and (n_out - n_rope_groups * RET_QK_DIM) % 512 == 0
    pos_blocks = seq // tm
    kern = functools.partial(_in_proj_kernel, n_rope_groups=n_rope_groups, n_q_groups=RET_HEADS,
                             k_scale=RET_QK_DIM ** -0.5, sub_m=sub_m, sub_n=512)
    whole = lambda a: pl.BlockSpec(a.shape, lambda i: (0, 0), pipeline_mode=pl.Buffered(1))
    return pl.pallas_call(
        kern,
        grid=(n // tm,),
        in_specs=[
            pl.BlockSpec((tm, d), lambda i: (i, 0)),
            whole(gain),
            whole(w),
            pl.BlockSpec((tm, RET_QK_DIM // 2), lambda i: (i % pos_blocks, 0)),
            pl.BlockSpec((tm, RET_QK_DIM // 2), lambda i: (i % pos_blocks, 0)),
        ],
        out_specs=pl.BlockSpec((tm, n_out), lambda i: (i, 0)),
        out_shape=jax.ShapeDtypeStruct((n, n_out), BF16),
        compiler_params=_params(("parallel",)),
        name="ret_in_proj",
    )(x2d, gain, w, cos, sin)


def _retention_kernel(q_ref, k_ref, v_ref, g_ref, dmask_ref, qdec_ref, kdec_ref, cdec_ref,
                      o_ref, state_ref, *, chunks_per_step):
    t = pl.program_id(1)

    @pl.when(t == 0)
    def _():
        state_ref[...] = jnp.zeros_like(state_ref)

    c = RET_CHUNK
    for ci in range(chunks_per_step):
        r0 = ci * c
        for h in range(RET_HEADS):
            q = q_ref[r0:r0 + c, h * RET_QK_DIM:(h + 1) * RET_QK_DIM]
            k = k_ref[r0:r0 + c, h * RET_QK_DIM:(h + 1) * RET_QK_DIM]
            v = v_ref[r0:r0 + c, h * RET_V_DIM:(h + 1) * RET_V_DIM]
            g = g_ref[r0:r0 + c, h * RET_V_DIM:(h + 1) * RET_V_DIM].astype(F32)
            state = state_ref[h]
            scores = lax.dot_general(q, k, (((1,), (1,)), ((), ())),
                                     preferred_element_type=F32) * dmask_ref[h]
            q_dec = (q.astype(F32) * qdec_ref[h]).astype(BF16)
            o = jnp.dot(jnp.concatenate([scores.astype(BF16), q_dec], axis=1),
                        jnp.concatenate([v, state.astype(BF16)], axis=0),
                        preferred_element_type=F32)
            k_dec = (k.astype(F32) * kdec_ref[h]).astype(BF16)
            state_ref[h] = state * cdec_ref[h] + lax.dot_general(
                k_dec, v, (((0,), (0,)), ((), ())), preferred_element_type=F32)
            mu = jnp.mean(o, axis=-1, keepdims=True)
            d = o - mu
            var = jnp.mean(d * d, axis=-1, keepdims=True)
            on = d * lax.rsqrt(var + EPS)
            o_ref[r0:r0 + c, h * RET_V_DIM:(h + 1) * RET_V_DIM] = (on * _silu(g)).astype(o_ref.dtype)


def _retention(proj, tables, *, batch, seq, rows=512):
    n = proj.shape[0]
    qw = RET_HEADS * RET_QK_DIM
    vw = RET_HEADS * RET_V_DIM
    assert proj.shape[1] == 2 * qw + 2 * vw and vw == 2 * qw
    assert seq % rows == 0 and rows % RET_CHUNK == 0
    steps = seq // rows
    dmask, qdec, kdec, cdec = tables
    kern = functools.partial(_retention_kernel, chunks_per_step=rows // RET_CHUNK)
    row = lambda b, t: b * steps + t
    whole = lambda a: pl.BlockSpec(a.shape, lambda b, t: (0,) * a.ndim)
    return pl.pallas_call(
        kern,
        grid=(batch, steps),
        in_specs=[
            pl.BlockSpec((rows, qw), lambda b, t: (row(b, t), 0)),
            pl.BlockSpec((rows, qw), lambda b, t: (row(b, t), 1)),
            pl.BlockSpec((rows, vw), lambda b, t: (row(b, t), 1)),
            pl.BlockSpec((rows, vw), lambda b, t: (row(b, t), 2)),
            whole(dmask), whole(qdec), whole(kdec), whole(cdec),
        ],
        out_specs=pl.BlockSpec((rows, vw), lambda b, t: (row(b, t), 0)),
        out_shape=jax.ShapeDtypeStruct((n, vw), BF16),
        scratch_shapes=[pltpu.VMEM((RET_HEADS, RET_QK_DIM, RET_V_DIM), F32)],
        compiler_params=_params(("parallel", "arbitrary")),
        name="retention",
    )(proj, proj, proj, proj, dmask, qdec, kdec, cdec)


def _retention_tables():
    c = RET_CHUNK
    log_gamma = jnp.log1p(-jnp.exp2(-5.0 - jnp.arange(RET_HEADS, dtype=F32)))
    idx = jnp.arange(c, dtype=F32)
    diff = idx[:, None] - idx[None, :]
    dmask = jnp.where(diff >= 0, jnp.exp(log_gamma[:, None, None] * diff), 0.0)
    qdec = jnp.exp(log_gamma[:, None] * (idx[None, :] + 1.0))[:, :, None]
    kdec = jnp.exp(log_gamma[:, None] * (c - 1.0 - idx[None, :]))[:, :, None]
    cdec = jnp.broadcast_to(jnp.exp(log_gamma * c)[:, None, None], (RET_HEADS, 1, RET_V_DIM))
    return dmask, qdec, kdec, cdec


def _out_proj_kernel(o_ref, w_ref, gain_ref, res_ref, *rest, scaled, sub_m, sub_n):
    n_next = len(scaled)
    nexts = []
    pos = 0
    for has_scale in scaled:
        width = 3 if has_scale else 2
        nexts.append(rest[pos:pos + width])
        pos += width
    h_ref = rest[pos]
    out_refs = rest[pos + 1:]
    assert len(out_refs) == n_next
    for r0 in range(0, o_ref.shape[0], sub_m):
        rs = slice(r0, r0 + sub_m)
        y = jnp.dot(o_ref[rs, :], w_ref[...], preferred_element_type=F32)
        h = res_ref[rs, :] + _rms_scale(y) * gain_ref[...]
        h_ref[rs, :] = h
        if n_next:
            hn = _rms_scale(h)
        for refs, out_ref in zip(nexts, out_refs):
            u = (hn * refs[0][...]).astype(BF16)
            for c0 in range(0, out_ref.shape[1], sub_n):
                cs = slice(c0, c0 + sub_n)
                acc = jnp.dot(u, refs[1][:, cs], preferred_element_type=F32)
                if len(refs) == 3:
                    acc = acc * refs[2][:, cs]
                out_ref[rs, cs] = acc.astype(out_ref.dtype)


def _out_proj(o, w, gain, res, nexts, *, tm, sub_m, name, sub_n=512):
    n, kdim = o.shape
    d = w.shape[1]
    row_spec = lambda width: pl.BlockSpec((tm, width), lambda i: (i, 0))
    whole = lambda a: pl.BlockSpec(a.shape, lambda i: (0, 0), pipeline_mode=pl.Buffered(1))
    operands = [o, w, gain, res]
    in_specs = [row_spec(kdim), whole(w), whole(gain), row_spec(d)]
    scaled = []
    out_shape = [jax.ShapeDtypeStruct((n, d), F32)]
    out_specs = [row_spec(d)]
    for gain_i, w_i, scale_i in nexts:
        group = [gain_i, w_i] + ([] if scale_i is None else [scale_i])
        operands += group
        in_specs += [whole(a) for a in group]
        scaled.append(scale_i is not None)
        out_shape.append(jax.ShapeDtypeStruct((n, w_i.shape[1]), BF16))
        out_specs.append(row_spec(w_i.shape[1]))
    return pl.pallas_call(
        functools.partial(_out_proj_kernel, scaled=tuple(scaled), sub_m=sub_m, sub_n=sub_n),
        grid=(n // tm,),
        in_specs=in_specs,
        out_specs=out_specs,
        out_shape=out_shape,
        compiler_params=_params(("parallel",)),
        name=name,
    )(*operands)


def _sb_kernel(pq_ref, pj_ref, q_ref, k_ref, v_ref, g_ref, mm_ref, o_ref,
               za0_ref, za1_ref, zb0_ref, zb1_ref, pba0_ref, pba1_ref, pbb0_ref, pbb1_ref,
               zpa0_ref, zpa1_ref, zpb0_ref, zpb1_ref, p0a0_ref, p0a1_ref, p0b0_ref, p0b1_ref,
               wa_ref, wb_ref, acc_ref, carry_ref, *, tq, tk, n_masked, n_pairs):
    z_refs = ((za0_ref, za1_ref), (zb0_ref, zb1_ref))
    pb_refs = ((pba0_ref, pba1_ref), (pbb0_ref, pbb1_ref))
    p0_refs = ((p0a0_ref, p0a1_ref), (p0b0_ref, p0b1_ref))
    zp_refs = ((zpa0_ref, zpa1_ref), (zpb0_ref, zpb1_ref))
    w_refs = (wa_ref, wb_ref)

    def rows(qi):
        return pl.ds(pl.multiple_of(qi * tq, tq), tq)

    def live_rows(masked, e):
        return slice(tk if (masked and e == 0) else 0, tq)

    def qk(p, par, masked):
        qi, jh = pq_ref[p], pj_ref[p]
        q = q_ref[rows(qi), :]
        for e in range(2):
            rs = live_rows(masked, e)
            kb = k_ref[pl.ds(pl.multiple_of((jh - e) * tk, tk), tk), :]
            z_refs[par][e][rs, :] = lax.dot_general(q[rs, :], kb, (((1,), (1,)), ((), ())),
                                                    preferred_element_type=F32)

    def score(p, par, masked):
        for e in range(2):
            rs = live_rows(masked, e)
            z2 = z_refs[par][e][rs, :]
            t = jnp.log2(1.0 + jnp.exp2(-jnp.abs(z2)))
            pos = jnp.maximum(z2, 0.0) + t
            zp = z2 - pos
            if masked:
                causal = (lax.broadcasted_iota(jnp.int32, z2.shape, 1)
                          < lax.broadcasted_iota(jnp.int32, z2.shape, 0))
                pos = jnp.where(causal, pos, 0.0)
                zp = jnp.where(causal, zp, MASKED_LOG2_WEIGHT)
            pb_refs[par][e][rs, :] = pos.astype(BF16)
            p0_refs[par][e][rs, :] = pos[:, 0:1]
            zp_refs[par][e][rs, :] = zp

    def weights(p, par, first):
        qi = pq_ref[p]
        carry = None if first else carry_ref[qi]
        for e in range(2):
            rs = live_rows(first, e)
            cols = slice((1 - e) * tk, (2 - e) * tk)
            sfx = jnp.dot(pb_refs[par][e][rs, :], mm_ref[...], preferred_element_type=F32)
            x = zp_refs[par][e][rs, :] - sfx
            if carry is not None:
                x = x - carry
            w_refs[par][rs, cols] = jnp.exp2(x).astype(BF16)
            row_sum = sfx[:, 0:1] + p0_refs[par][e][rs, :]
            if rs.start:
                w_refs[par][0:rs.start, cols] = jnp.zeros((rs.start, tk), BF16)
                row_sum = jnp.concatenate([jnp.zeros((rs.start, 1), F32), row_sum], axis=0)
            carry = row_sum if carry is None else carry + row_sum
        carry_ref[qi] = carry

    def pv(p, par, first):
        qi, jh = pq_ref[p], pj_ref[p]
        vb = v_ref[pl.ds(pl.multiple_of((jh - 1) * tk, tk), 2 * tk), :]
        out = jnp.dot(w_refs[par][...], vb, preferred_element_type=F32)
        if first:
            acc_ref[qi] = out
        else:
            acc_ref[qi] += out

    def body(b, par, masked_of):
        if masked_of(2) is not None:
            weights(b - 2, par, first=masked_of(2))
        if masked_of(3) is not None:
            pv(b - 3, 1 - par, first=masked_of(3))
        if masked_of(0) is not None:
            qk(b, par, masked_of(0))
        if masked_of(1) is not None:
            score(b - 1, 1 - par, masked=masked_of(1))

    def static_flags(b):
        def masked_of(s):
            p = b - s
            return None if not 0 <= p < n_pairs else p < n_masked
        return masked_of

    def peel(lo, hi):
        for b in range(lo, hi):
            body(b, b % 2, static_flags(b))

    def loop(lo, hi, flag, unroll):
        assert (hi - lo) % unroll == 0 and unroll % 2 == 0

        def step(m, c):
            for d in range(unroll):
                body(lo + unroll * m + d, (lo + d) % 2, lambda s: flag)
            return c
        lax.fori_loop(0, (hi - lo) // unroll, step, 0)

    depth = 3
    m_unroll, u_unroll = 2, 4
    m_lo = depth + (n_masked - depth) % m_unroll
    u_lo = n_masked + depth + (n_pairs - n_masked - depth) % u_unroll
    assert m_lo <= n_masked and u_lo <= n_pairs
    peel(0, m_lo)
    loop(m_lo, n_masked, True, m_unroll)
    peel(n_masked, u_lo)
    loop(u_lo, n_pairs, False, u_unroll)
    peel(n_pairs, n_pairs + depth)

    for qi in range(acc_ref.shape[0]):
        g = g_ref[qi * tq:(qi + 1) * tq, :].astype(F32)
        o_ref[qi * tq:(qi + 1) * tq, :] = (acc_ref[qi] * _silu(g)).astype(o_ref.dtype)


def _sb_attention(qg, kv, *, batch, seq, tq=512, tk=256):
    n = qg.shape[0]
    qkw = SB_HEADS * SB_QK_DIM
    vw = SB_HEADS * SB_V_DIM
    assert qg.shape[1] == qkw + vw and kv.shape[1] == qkw + vw and vw == 2 * qkw
    assert seq % tq == 0 and tq == 2 * tk
    nq = seq // tq
    v_col0 = qkw // SB_V_DIM
    rr = lax.broadcasted_iota(jnp.int32, (tk, tk), 0)
    cc = lax.broadcasted_iota(jnp.int32, (tk, tk), 1)
    suffix_mat = (rr > cc).astype(BF16)
    pairs = [(qi, 2 * qi + 1) for qi in range(nq)]
    pairs += [(qi, jh) for qi in range(nq) for jh in range(2 * qi - 1, 0, -2)]
    tables = [jnp.asarray(col, jnp.int32) for col in zip(*pairs)]
    kern = functools.partial(_sb_kernel, tq=tq, tk=tk, n_masked=nq, n_pairs=len(pairs))
    stage_f32 = pltpu.VMEM((tq, tk), F32)
    stage_bf16 = pltpu.VMEM((tq, tk), BF16)
    stage_col = pltpu.VMEM((tq, 1), F32)
    return pl.pallas_call(
        kern,
        grid_spec=pltpu.PrefetchScalarGridSpec(
            num_scalar_prefetch=len(tables),
            grid=(batch, SB_HEADS),
            in_specs=[
                pl.BlockSpec((seq, SB_QK_DIM), lambda b, h, *_: (b, h)),
                pl.BlockSpec((seq, SB_QK_DIM), lambda b, h, *_: (b, h)),
                pl.BlockSpec((seq, SB_V_DIM), lambda b, h, *_: (b, v_col0 + h)),
                pl.BlockSpec((seq, SB_V_DIM), lambda b, h, *_: (b, v_col0 + h)),
                pl.BlockSpec((tk, tk), lambda b, h, *_: (0, 0)),
            ],
            out_specs=pl.BlockSpec((seq, SB_V_DIM), lambda b, h, *_: (b, h)),
            scratch_shapes=[stage_f32] * 4 + [stage_bf16] * 4 + [stage_f32] * 4 + [stage_col] * 4
            + [pltpu.VMEM((tq, 2 * tk), BF16)] * 2 + [
                pltpu.VMEM((nq, tq, SB_V_DIM), F32), pltpu.VMEM((nq, tq, 1), F32)],
        ),
        out_shape=jax.ShapeDtypeStruct((n, vw), BF16),
        compiler_params=_params(("parallel", "parallel")),
        name="sb_attention",
    )(*tables, qg, kv, kv, qg, suffix_mat)


def kernel(x, ret_norm_pre, ret_w_in, ret_w_out, ret_norm_post, kv_norm, w_kv,
           sb_norm_pre, sb_w_in, sb_w_out, sb_norm_post):
    batch, seq, d = x.shape
    assert ret_w_in.shape[0] == 1 and sb_w_in.shape[0] == 1
    n = batch * seq
    x2d = x.reshape(n, d)

    pos = jnp.arange(seq, dtype=F32)
    half = RET_QK_DIM // 2
    inv_freq = 1.0 / (ROPE_BASE ** (jnp.arange(half, dtype=F32) / half))
    ang = pos[:, None] * inv_freq[None, :]
    cos, sin = jnp.cos(ang), jnp.sin(ang)

    vec = lambda g: g.reshape(1, d).astype(F32)

    proj = _in_proj(x2d, vec(ret_norm_pre[0]), ret_w_in[0].astype(BF16), cos, sin, seq=seq)
    o = _retention(proj, _retention_tables(), batch=batch, seq=seq)
    qkw = SB_HEADS * SB_QK_DIM
    q_scale = math.log2(math.e) / math.sqrt(SB_QK_DIM)
    col_scale = jnp.ones((1, sb_w_in.shape[2]), F32).at[:, :qkw].set(q_scale)
    h1, kv, qg = _out_proj(o, ret_w_out[0].astype(BF16), vec(ret_norm_post[0]), x2d,
                           [(vec(kv_norm), w_kv.astype(BF16), None),
                            (vec(sb_norm_pre[0]), sb_w_in[0].astype(BF16), col_scale)],
                           tm=512, sub_m=256, name="ret_out_kv_q_proj")
    o2 = _sb_attention(qg, kv, batch=batch, seq=seq)
    (out,) = _out_proj(o2, sb_w_out[0].astype(BF16), vec(sb_norm_post[0]), h1, [],
                       tm=1024, sub_m=256, name="sb_out_proj")
    return out.reshape(batch, seq, d)
```

```python
import functools
import math

import jax
import jax.numpy as jnp
from jax import lax
from jax.experimental import pallas as pl
from jax.experimental.pallas import tpu as pltpu

EPS = 1e-6
ROPE_BASE = 10000.0

RET_HEADS = 4
RET_QK_DIM = 256
RET_V_DIM = 512
RET_CHUNK = 256

SB_HEADS = 8
SB_QK_DIM = 128
SB_V_DIM = 256

VMEM_LIMIT_BYTES = 56 * 1024 * 1024

MASKED_LOG2_WEIGHT = -1e30

F32 = jnp.float32
BF16 = jnp.bfloat16


def _params(semantics):
    return pltpu.CompilerParams(dimension_semantics=semantics,
                                vmem_limit_bytes=VMEM_LIMIT_BYTES)


def _rms_scale(x):
    return x * lax.rsqrt(jnp.mean(x * x, axis=-1, keepdims=True) + EPS)


def _silu(g):
    return g / (1.0 + jnp.exp2(g * (-math.log2(math.e))))


def _in_proj_kernel(x_ref, gain_ref, w_ref, cos_ref, sin_ref, o_ref, *,
                    n_rope_groups, n_q_groups, k_scale, sub_m, sub_n):
    half = RET_QK_DIM // 2
    rope_cols = n_rope_groups * RET_QK_DIM
    for r0 in range(0, o_ref.shape[0], sub_m):
        rs = slice(r0, r0 + sub_m)
        u = (_rms_scale(x_ref[rs, :]) * gain_ref[...]).astype(BF16)
        cos = cos_ref[rs, :]
        sin = sin_ref[rs, :]
        for grp in range(n_rope_groups):
            c0 = grp * RET_QK_DIM
            acc = jnp.dot(u, w_ref[:, c0:c0 + RET_QK_DIM], preferred_element_type=F32)
            if grp >= n_q_groups:
                acc = acc * k_scale
            x1 = acc[:, :half]
            x2 = acc[:, half:]
            o_ref[rs, c0:c0 + half] = (x1 * cos - x2 * sin).astype(o_ref.dtype)
            o_ref[rs, c0 + half:c0 + RET_QK_DIM] = (x1 * sin + x2 * cos).astype(o_ref.dtype)
        for c0 in range(rope_cols, o_ref.shape[1], sub_n):
            o_ref[rs, c0:c0 + sub_n] = jnp.dot(
                u, w_ref[:, c0:c0 + sub_n], preferred_element_type=F32).astype(o_ref.dtype)


def _in_proj(x2d, gain, w, cos, sin, *, seq, tm=512, sub_m=256):
    n, d = x2d.shape
    n_out = w.shape[1]
    n_rope_groups = 2 * RET_HEADS
    assert n % tm == 0 and seq % tm == 0 and (n_out - n_rope_groups * RET_QK_DIM) % 512 == 0
    pos_blocks = seq // tm
    kern = functools.partial(_in_proj_kernel, n_rope_groups=n_rope_groups, n_q_groups=RET_HEADS,
                             k_scale=RET_QK_DIM ** -0.5, sub_m=sub_m, sub_n=512)
    whole = lambda a: pl.BlockSpec(a.shape, lambda i: (0, 0), pipeline_mode=pl.Buffered(1))
    return pl.pallas_call(
        kern,
        grid=(n // tm,),
        in_specs=[
            pl.BlockSpec((tm, d), lambda i: (i, 0)),
            whole(gain),
            whole(w),
            pl.BlockSpec((tm, RET_QK_DIM // 2), lambda i: (i % pos_blocks, 0)),
            pl.BlockSpec((tm, RET_QK_DIM // 2), lambda i: (i % pos_blocks, 0)),
        ],
        out_specs=pl.BlockSpec((tm, n_out), lambda i: (i, 0)),
        out_shape=jax.ShapeDtypeStruct((n, n_out), BF16),
        compiler_params=_params(("parallel",)),
        name="ret_in_proj",
    )(x2d, gain, w, cos, sin)


def _retention_kernel(q_ref, k_ref, v_ref, dmask_ref, qdec_ref, kdec_ref, cdec_ref,
                      o_ref, state_ref, *, chunks_per_step):
    t = pl.program_id(1)

    @pl.when(t == 0)
    def _():
        state_ref[...] = jnp.zeros_like(state_ref)

    c = RET_CHUNK
    for ci in range(chunks_per_step):
        r0 = ci * c
        for h in range(RET_HEADS):
            q = q_ref[r0:r0 + c, h * RET_QK_DIM:(h + 1) * RET_QK_DIM]
            k = k_ref[r0:r0 + c, h * RET_QK_DIM:(h + 1) * RET_QK_DIM]
            v = v_ref[r0:r0 + c, h * RET_V_DIM:(h + 1) * RET_V_DIM]
            state = state_ref[h]
            scores = lax.dot_general(q, k, (((1,), (1,)), ((), ())),
                                     preferred_element_type=F32) * dmask_ref[h]
            q_dec = (q.astype(F32) * qdec_ref[h]).astype(BF16)
            o = jnp.dot(jnp.concatenate([scores.astype(BF16), q_dec], axis=1),
                        jnp.concatenate([v, state.astype(BF16)], axis=0),
                        preferred_element_type=F32)
            k_dec = (k.astype(F32) * kdec_ref[h]).astype(BF16)
            state_ref[h] = state * cdec_ref[h] + lax.dot_general(
                k_dec, v, (((0,), (0,)), ((), ())), preferred_element_type=F32)
            mu = jnp.mean(o, axis=-1, keepdims=True)
            d = o - mu
            var = jnp.mean(d * d, axis=-1, keepdims=True)
            on = d * lax.rsqrt(var + EPS)
            o_ref[r0:r0 + c, h * RET_V_DIM:(h + 1) * RET_V_DIM] = on.astype(o_ref.dtype)


def _retention(proj, tables, *, batch, seq, rows=512):
    n = proj.shape[0]
    qw = RET_HEADS * RET_QK_DIM
    vw = RET_HEADS * RET_V_DIM
    assert proj.shape[1] == 2 * qw + 2 * vw and vw == 2 * qw
    assert seq % rows == 0 and rows % RET_CHUNK == 0
    steps = seq // rows
    dmask, qdec, kdec, cdec = tables
    kern = functools.partial(_retention_kernel, chunks_per_step=rows // RET_CHUNK)
    row = lambda b, t: b * steps + t
    whole = lambda a: pl.BlockSpec(a.shape, lambda b, t: (0,) * a.ndim)
    return pl.pallas_call(
        kern,
        grid=(batch, steps),
        in_specs=[
            pl.BlockSpec((rows, qw), lambda b, t: (row(b, t), 0)),
            pl.BlockSpec((rows, qw), lambda b, t: (row(b, t), 1)),
            pl.BlockSpec((rows, vw), lambda b, t: (row(b, t), 1)),
            whole(dmask), whole(qdec), whole(kdec), whole(cdec),
        ],
        out_specs=pl.BlockSpec((rows, vw), lambda b, t: (row(b, t), 0)),
        out_shape=jax.ShapeDtypeStruct((n, vw), BF16),
        scratch_shapes=[pltpu.VMEM((RET_HEADS, RET_QK_DIM, RET_V_DIM), F32)],
        compiler_params=_params(("parallel", "arbitrary")),
        name="retention",
    )(proj, proj, proj, dmask, qdec, kdec, cdec)


def _retention_tables():
    c = RET_CHUNK
    log_gamma = jnp.log1p(-jnp.exp2(-5.0 - jnp.arange(RET_HEADS, dtype=F32)))
    idx = jnp.arange(c, dtype=F32)
    diff = idx[:, None] - idx[None, :]
    dmask = jnp.where(diff >= 0, jnp.exp(log_gamma[:, None, None] * diff), 0.0)
    qdec = jnp.exp(log_gamma[:, None] * (idx[None, :] + 1.0))[:, :, None]
    kdec = jnp.exp(log_gamma[:, None] * (c - 1.0 - idx[None, :]))[:, :, None]
    cdec = jnp.broadcast_to(jnp.exp(log_gamma * c)[:, None, None], (RET_HEADS, 1, RET_V_DIM))
    return dmask, qdec, kdec, cdec


def _out_proj_kernel(o_ref, *rest, n_gate, scaled, sub_m, sub_n):
    gate_refs = rest[:n_gate]
    w_ref, gain_ref, res_ref = rest[n_gate:n_gate + 3]
    rest = rest[n_gate + 3:]
    n_next = len(scaled)
    nexts = []
    pos = 0
    for has_scale in scaled:
        width = 3 if has_scale else 2
        nexts.append(rest[pos:pos + width])
        pos += width
    h_ref = rest[pos]
    out_refs = rest[pos + 1:]
    assert len(out_refs) == n_next
    for r0 in range(0, o_ref.shape[0], sub_m):
        rs = slice(r0, r0 + sub_m)
        gated = []
        c0 = 0
        for g_ref in gate_refs:
            cs = slice(c0, c0 + g_ref.shape[1])
            gated.append((o_ref[rs, cs].astype(F32) * _silu(g_ref[rs, :].astype(F32))).astype(BF16))
            c0 = cs.stop
        assert c0 == o_ref.shape[1]
        y = jnp.dot(jnp.concatenate(gated, axis=1), w_ref[...], preferred_element_type=F32)
        h = res_ref[rs, :] + _rms_scale(y) * gain_ref[...]
        h_ref[rs, :] = h
        if n_next:
            hn = _rms_scale(h)
        for refs, out_ref in zip(nexts, out_refs):
            u = (hn * refs[0][...]).astype(BF16)
            for c0 in range(0, out_ref.shape[1], sub_n):
                cs = slice(c0, c0 + sub_n)
                acc = jnp.dot(u, refs[1][:, cs], preferred_element_type=F32)
                if len(refs) == 3:
                    acc = acc * refs[2][:, cs]
                out_ref[rs, cs] = acc.astype(out_ref.dtype)


def _out_proj(o, gate, w, gain, res, nexts, *, tm, sub_m, name, sub_n=512):
    n, kdim = o.shape
    d = w.shape[1]
    row_spec = lambda width, col=0: pl.BlockSpec((tm, width), lambda i: (i, col))
    whole = lambda a: pl.BlockSpec(a.shape, lambda i: (0, 0), pipeline_mode=pl.Buffered(1))
    gate_arr, gate_width, gate_block0 = gate
    n_gate = kdim // gate_width
    assert n_gate * gate_width == kdim
    operands = [o] + [gate_arr] * n_gate + [w, gain, res]
    in_specs = ([row_spec(kdim)] + [row_spec(gate_width, gate_block0 + t) for t in range(n_gate)]
                + [whole(w), whole(gain), row_spec(d)])
    scaled = []
    out_shape = [jax.ShapeDtypeStruct((n, d), F32)]
    out_specs = [row_spec(d)]
    for gain_i, w_i, scale_i in nexts:
        group = [gain_i, w_i] + ([] if scale_i is None else [scale_i])
        operands += group
        in_specs += [whole(a) for a in group]
        scaled.append(scale_i is not None)
        out_shape.append(jax.ShapeDtypeStruct((n, w_i.shape[1]), BF16))
        out_specs.append(row_spec(w_i.shape[1]))
    return pl.pallas_call(
        functools.partial(_out_proj_kernel, n_gate=n_gate, scaled=tuple(scaled), sub_m=sub_m,
                          sub_n=sub_n),
        grid=(n // tm,),
        in_specs=in_specs,
        out_specs=out_specs,
        out_shape=out_shape,
        compiler_params=_params(("parallel",)),
        name=name,
    )(*operands)


def _sb_kernel(pq_ref, pj_ref, q_ref, k_ref, v_ref, mm_ref, o_ref,
               za0_ref, za1_ref, zb0_ref, zb1_ref, pba0_ref, pba1_ref, pbb0_ref, pbb1_ref,
               zpa0_ref, zpa1_ref, zpb0_ref, zpb1_ref, p0a0_ref, p0a1_ref, p0b0_ref, p0b1_ref,
               wa_ref, wb_ref, acc_ref, carry_ref, *, tq, tk, n_masked, n_pairs):
    z_refs = ((za0_ref, za1_ref), (zb0_ref, zb1_ref))
    pb_refs = ((pba0_ref, pba1_ref), (pbb0_ref, pbb1_ref))
    p0_refs = ((p0a0_ref, p0a1_ref), (p0b0_ref, p0b1_ref))
    zp_refs = ((zpa0_ref, zpa1_ref), (zpb0_ref, zpb1_ref))
    w_refs = (wa_ref, wb_ref)

    def rows(qi):
        return pl.ds(pl.multiple_of(qi * tq, tq), tq)

    def live_rows(masked, e):
        return slice(tk if (masked and e == 0) else 0, tq)

    def qk(p, par, masked):
        qi, jh = pq_ref[p], pj_ref[p]
        q = q_ref[rows(qi), :]
        for e in range(2):
            rs = live_rows(masked, e)
            kb = k_ref[pl.ds(pl.multiple_of((jh - e) * tk, tk), tk), :]
            z_refs[par][e][rs, :] = lax.dot_general(q[rs, :], kb, (((1,), (1,)), ((), ())),
                                                    preferred_element_type=F32)

    def score(p, par, masked):
        for e in range(2):
            rs = live_rows(masked, e)
            z2 = z_refs[par][e][rs, :]
            t = jnp.log2(1.0 + jnp.exp2(-jnp.abs(z2)))
            pos = jnp.maximum(z2, 0.0) + t
            zp = z2 - pos
            if masked:
                causal = (lax.broadcasted_iota(jnp.int32, z2.shape, 1)
                          < lax.broadcasted_iota(jnp.int32, z2.shape, 0))
                pos = jnp.where(causal, pos, 0.0)
                zp = jnp.where(causal, zp, MASKED_LOG2_WEIGHT)
            pb_refs[par][e][rs, :] = pos.astype(BF16)
            p0_refs[par][e][rs, :] = pos[:, 0:1]
            zp_refs[par][e][rs, :] = zp

    def weights(p, par, first):
        qi = pq_ref[p]
        carry = None if first else carry_ref[qi]
        for e in range(2):
            rs = live_rows(first, e)
            cols = slice((1 - e) * tk, (2 - e) * tk)
            sfx = jnp.dot(pb_refs[par][e][rs, :], mm_ref[...], preferred_element_type=F32)
            x = zp_refs[par][e][rs, :] - sfx
            if carry is not None:
                x = x - carry
            w_refs[par][rs, cols] = jnp.exp2(x).astype(BF16)
            row_sum = sfx[:, 0:1] + p0_refs[par][e][rs, :]
            if rs.start:
                w_refs[par][0:rs.start, cols] = jnp.zeros((rs.start, tk), BF16)
                row_sum = jnp.concatenate([jnp.zeros((rs.start, 1), F32), row_sum], axis=0)
            carry = row_sum if carry is None else carry + row_sum
        carry_ref[qi] = carry

    def pv(p, par, first):
        qi, jh = pq_ref[p], pj_ref[p]
        vb = v_ref[pl.ds(pl.multiple_of((jh - 1) * tk, tk), 2 * tk), :]
        out = jnp.dot(w_refs[par][...], vb, preferred_element_type=F32)
        if first:
            acc_ref[qi] = out
        else:
            acc_ref[qi] += out

    def body(b, par, masked_of):
        if masked_of(2) is not None:
            weights(b - 2, par, first=masked_of(2))
        if masked_of(3) is not None:
            pv(b - 3, 1 - par, first=masked_of(3))
        if masked_of(0) is not None:
            qk(b, par, masked_of(0))
        if masked_of(1) is not None:
            score(b - 1, 1 - par, masked=masked_of(1))

    def static_flags(b):
        def masked_of(s):
            p = b - s
            return None if not 0 <= p < n_pairs else p < n_masked
        return masked_of

    def peel(lo, hi):
        for b in range(lo, hi):
            body(b, b % 2, static_flags(b))

    def loop(lo, hi, flag, unroll):
        assert (hi - lo) % unroll == 0 and unroll % 2 == 0

        def step(m, c):
            for d in range(unroll):
                body(lo + unroll * m + d, (lo + d) % 2, lambda s: flag)
            return c
        lax.fori_loop(0, (hi - lo) // unroll, step, 0)

    depth = 3
    m_unroll, u_unroll = 2, 8
    m_lo = depth + (n_masked - depth) % m_unroll
    u_lo = n_masked + depth + (n_pairs - n_masked - depth) % u_unroll
    assert m_lo <= n_masked and u_lo <= n_pairs
    peel(0, m_lo)
    loop(m_lo, n_masked, True, m_unroll)
    peel(n_masked, u_lo)
    loop(u_lo, n_pairs, False, u_unroll)
    peel(n_pairs, n_pairs + depth)

    for qi in range(acc_ref.shape[0]):
        o_ref[qi * tq:(qi + 1) * tq, :] = acc_ref[qi].astype(o_ref.dtype)


def _sb_attention(qg, kv, *, batch, seq, tq=512, tk=256):
    n = qg.shape[0]
    qkw = SB_HEADS * SB_QK_DIM
    vw = SB_HEADS * SB_V_DIM
    assert qg.shape[1] == qkw + vw and kv.shape[1] == qkw + vw and vw == 2 * qkw
    assert seq % tq == 0 and tq == 2 * tk
    nq = seq // tq
    v_col0 = qkw // SB_V_DIM
    rr = lax.broadcasted_iota(jnp.int32, (tk, tk), 0)
    cc = lax.broadcasted_iota(jnp.int32, (tk, tk), 1)
    suffix_mat = (rr > cc).astype(BF16)
    pairs = [(qi, 2 * qi + 1) for qi in range(nq)]
    pairs += [(qi, jh) for qi in range(nq) for jh in range(2 * qi - 1, 0, -2)]
    tables = [jnp.asarray(col, jnp.int32) for col in zip(*pairs)]
    kern = functools.partial(_sb_kernel, tq=tq, tk=tk, n_masked=nq, n_pairs=len(pairs))
    stage_f32 = pltpu.VMEM((tq, tk), F32)
    stage_bf16 = pltpu.VMEM((tq, tk), BF16)
    stage_col = pltpu.VMEM((tq, 1), F32)
    return pl.pallas_call(
        kern,
        grid_spec=pltpu.PrefetchScalarGridSpec(
            num_scalar_prefetch=len(tables),
            grid=(batch, SB_HEADS),
            in_specs=[
                pl.BlockSpec((seq, SB_QK_DIM), lambda b, h, *_: (b, h)),
                pl.BlockSpec((seq, SB_QK_DIM), lambda b, h, *_: (b, h)),
                pl.BlockSpec((seq, SB_V_DIM), lambda b, h, *_: (b, v_col0 + h)),
                pl.BlockSpec((tk, tk), lambda b, h, *_: (0, 0)),
            ],
            out_specs=pl.BlockSpec((seq, SB_V_DIM), lambda b, h, *_: (b, h)),
            scratch_shapes=[stage_f32] * 4 + [stage_bf16] * 4 + [stage_f32] * 4 + [stage_col] * 4
            + [pltpu.VMEM((tq, 2 * tk), BF16)] * 2 + [
                pltpu.VMEM((nq, tq, SB_V_DIM), F32), pltpu.VMEM((nq, tq, 1), F32)],
        ),
        out_shape=jax.ShapeDtypeStruct((n, vw), BF16),
        compiler_params=_params(("parallel", "parallel")),
        name="sb_attention",
    )(*tables, qg, kv, kv, suffix_mat)


def kernel(x, ret_norm_pre, ret_w_in, ret_w_out, ret_norm_post, kv_norm, w_kv,
           sb_norm_pre, sb_w_in, sb_w_out, sb_norm_post):
    batch, seq, d = x.shape
    assert ret_w_in.shape[0] == 1 and sb_w_in.shape[0] == 1
    n = batch * seq
    x2d = x.reshape(n, d)

    pos = jnp.arange(seq, dtype=F32)
    half = RET_QK_DIM // 2
    inv_freq = 1.0 / (ROPE_BASE ** (jnp.arange(half, dtype=F32) / half))
    ang = pos[:, None] * inv_freq[None, :]
    cos, sin = jnp.cos(ang), jnp.sin(ang)

    vec = lambda g: g.reshape(1, d).astype(F32)

    proj = _in_proj(x2d, vec(ret_norm_pre[0]), ret_w_in[0].astype(BF16), cos, sin, seq=seq)
    o = _retention(proj, _retention_tables(), batch=batch, seq=seq)
    qkw = SB_HEADS * SB_QK_DIM
    q_scale = math.log2(math.e) / math.sqrt(SB_QK_DIM)
    col_scale = jnp.ones((1, sb_w_in.shape[2]), F32).at[:, :qkw].set(q_scale)
    ret_width = RET_HEADS * RET_V_DIM
    h1, kv, qg = _out_proj(o, (proj, ret_width, 2), ret_w_out[0].astype(BF16),
                           vec(ret_norm_post[0]), x2d,
                           [(vec(kv_norm), w_kv.astype(BF16), None),
                            (vec(sb_norm_pre[0]), sb_w_in[0].astype(BF16), col_scale)],
                           tm=512, sub_m=512, name="ret_out_kv_q_proj")
    o2 = _sb_attention(qg, kv, batch=batch, seq=seq)
    (out,) = _out_proj(o2, (qg, qkw, 1), sb_w_out[0].astype(BF16), vec(sb_norm_post[0]), h1, [],
                       tm=1024, sub_m=256, name="sb_out_proj")
    return out.reshape(batch, seq, d)
```

```python
import functools
import math

import jax
import jax.numpy as jnp
from jax import lax
from jax.experimental import pallas as pl
from jax.experimental.pallas import tpu as pltpu

EPS = 1e-6
ROPE_BASE = 10000.0

RET_HEADS = 4
RET_QK_DIM = 256
RET_V_DIM = 512
RET_CHUNK = 256

SB_HEADS = 8
SB_QK_DIM = 128
SB_V_DIM = 256

VMEM_LIMIT_BYTES = 56 * 1024 * 1024

MASKED_LOG2_WEIGHT = -1e30

F32 = jnp.float32
BF16 = jnp.bfloat16


def _params(semantics):
    return pltpu.CompilerParams(dimension_semantics=semantics,
                                vmem_limit_bytes=VMEM_LIMIT_BYTES)


def _rms_scale(x):
    return x * lax.rsqrt(jnp.mean(x * x, axis=-1, keepdims=True) + EPS)


def _silu(g):
    return g / (1.0 + jnp.exp2(g * (-math.log2(math.e))))


def _in_proj_kernel(x_ref, gain_ref, w_ref, cos_ref, sin_ref, o_ref, *,
                    n_rope_groups, n_q_groups, k_scale, sub_m, sub_n):
    half = RET_QK_DIM // 2
    rope_cols = n_rope_groups * RET_QK_DIM
    for r0 in range(0, o_ref.shape[0], sub_m):
        rs = slice(r0, r0 + sub_m)
        u = (_rms_scale(x_ref[rs, :]) * gain_ref[...]).astype(BF16)
        cos = cos_ref[rs, :]
        sin = sin_ref[rs, :]
        for grp in range(n_rope_groups):
            c0 = grp * RET_QK_DIM
            acc = jnp.dot(u, w_ref[:, c0:c0 + RET_QK_DIM], preferred_element_type=F32)
            if grp >= n_q_groups:
                acc = acc * k_scale
            x1 = acc[:, :half]
            x2 = acc[:, half:]
            o_ref[rs, c0:c0 + half] = (x1 * cos - x2 * sin).astype(o_ref.dtype)
            o_ref[rs, c0 + half:c0 + RET_QK_DIM] = (x1 * sin + x2 * cos).astype(o_ref.dtype)
        for c0 in range(rope_cols, o_ref.shape[1], sub_n):
            o_ref[rs, c0:c0 + sub_n] = jnp.dot(
                u, w_ref[:, c0:c0 + sub_n], preferred_element_type=F32).astype(o_ref.dtype)


def _in_proj(x2d, gain, w, cos, sin, *, seq, tm=512, sub_m=256):
    n, d = x2d.shape
    n_out = w.shape[1]
    n_rope_groups = 2 * RET_HEADS
    assert n % tm == 0 and seq % tm == 0 and (n_out - n_rope_groups * RET_QK_DIM) % 512 == 0
    pos_blocks = seq // tm
    kern = functools.partial(_in_proj_kernel, n_rope_groups=n_rope_groups, n_q_groups=RET_HEADS,
                             k_scale=RET_QK_DIM ** -0.5, sub_m=sub_m, sub_n=512)
    whole = lambda a: pl.BlockSpec(a.shape, lambda i: (0, 0), pipeline_mode=pl.Buffered(1))
    return pl.pallas_call(
        kern,
        grid=(n // tm,),
        in_specs=[
            pl.BlockSpec((tm, d), lambda i: (i, 0)),
            whole(gain),
            whole(w),
            pl.BlockSpec((tm, RET_QK_DIM // 2), lambda i: (i % pos_blocks, 0)),
            pl.BlockSpec((tm, RET_QK_DIM // 2), lambda i: (i % pos_blocks, 0)),
        ],
        out_specs=pl.BlockSpec((tm, n_out), lambda i: (i, 0)),
        out_shape=jax.ShapeDtypeStruct((n, n_out), BF16),
        compiler_params=_params(("parallel",)),
        name="ret_in_proj",
    )(x2d, gain, w, cos, sin)


def _retention_kernel(q_ref, k_ref, v_ref, dmask_ref, qdec_ref, kdec_ref, cdec_ref,
                      o_ref, state_ref, *, chunks_per_step):
    t = pl.program_id(1)

    @pl.when(t == 0)
    def _():
        state_ref[...] = jnp.zeros_like(state_ref)

    c = RET_CHUNK
    for ci in range(chunks_per_step):
        r0 = ci * c
        for h in range(RET_HEADS):
            q = q_ref[r0:r0 + c, h * RET_QK_DIM:(h + 1) * RET_QK_DIM]
            k = k_ref[r0:r0 + c, h * RET_QK_DIM:(h + 1) * RET_QK_DIM]
            v = v_ref[r0:r0 + c, h * RET_V_DIM:(h + 1) * RET_V_DIM]
            state = state_ref[h]
            scores = lax.dot_general(q, k, (((1,), (1,)), ((), ())),
                                     preferred_element_type=F32) * dmask_ref[h]
            q_dec = (q.astype(F32) * qdec_ref[h]).astype(BF16)
            o = jnp.dot(jnp.concatenate([scores.astype(BF16), q_dec], axis=1),
                        jnp.concatenate([v, state.astype(BF16)], axis=0),
                        preferred_element_type=F32)
            k_dec = (k.astype(F32) * kdec_ref[h]).astype(BF16)
            state_ref[h] = state * cdec_ref[h] + lax.dot_general(
                k_dec, v, (((0,), (0,)), ((), ())), preferred_element_type=F32)
            mu = jnp.mean(o, axis=-1, keepdims=True)
            d = o - mu
            var = jnp.mean(d * d, axis=-1, keepdims=True)
            on = d * lax.rsqrt(var + EPS)
            o_ref[r0:r0 + c, h * RET_V_DIM:(h + 1) * RET_V_DIM] = on.astype(o_ref.dtype)


def _retention(proj, tables, *, batch, seq, rows=512):
    n = proj.shape[0]
    qw = RET_HEADS * RET_QK_DIM
    vw = RET_HEADS * RET_V_DIM
    assert proj.shape[1] == 2 * qw + 2 * vw and vw == 2 * qw
    assert seq % rows == 0 and rows % RET_CHUNK == 0
    steps = seq // rows
    dmask, qdec, kdec, cdec = tables
    kern = functools.partial(_retention_kernel, chunks_per_step=rows // RET_CHUNK)
    row = lambda b, t: b * steps + t
    whole = lambda a: pl.BlockSpec(a.shape, lambda b, t: (0,) * a.ndim)
    return pl.pallas_call(
        kern,
        grid=(batch, steps),
        in_specs=[
            pl.BlockSpec((rows, qw), lambda b, t: (row(b, t), 0)),
            pl.BlockSpec((rows, qw), lambda b, t: (row(b, t), 1)),
            pl.BlockSpec((rows, vw), lambda b, t: (row(b, t), 1)),
            whole(dmask), whole(qdec), whole(kdec), whole(cdec),
        ],
        out_specs=pl.BlockSpec((rows, vw), lambda b, t: (row(b, t), 0)),
        out_shape=jax.ShapeDtypeStruct((n, vw), BF16),
        scratch_shapes=[pltpu.VMEM((RET_HEADS, RET_QK_DIM, RET_V_DIM), F32)],
        compiler_params=_params(("parallel", "arbitrary")),
        name="retention",
    )(proj, proj, proj, dmask, qdec, kdec, cdec)


def _retention_tables():
    c = RET_CHUNK
    log_gamma = jnp.log1p(-jnp.exp2(-5.0 - jnp.arange(RET_HEADS, dtype=F32)))
    idx = jnp.arange(c, dtype=F32)
    diff = idx[:, None] - idx[None, :]
    dmask = jnp.where(diff >= 0, jnp.exp(log_gamma[:, None, None] * diff), 0.0)
    qdec = jnp.exp(log_gamma[:, None] * (idx[None, :] + 1.0))[:, :, None]
    kdec = jnp.exp(log_gamma[:, None] * (c - 1.0 - idx[None, :]))[:, :, None]
    cdec = jnp.broadcast_to(jnp.exp(log_gamma * c)[:, None, None], (RET_HEADS, 1, RET_V_DIM))
    return dmask, qdec, kdec, cdec


def _out_proj_kernel(o_ref, *rest, n_gate, layout, sub_m, sub_n):
    gate_refs = rest[:n_gate]
    w_ref, gain_ref, res_ref = rest[n_gate:n_gate + 3]
    rest = rest[n_gate + 3:]
    nexts = []
    pos = 0
    for has_scale, _ in layout:
        width = 3 if has_scale else 2
        nexts.append(rest[pos:pos + width])
        pos += width
    h_ref = rest[pos]
    out_refs = list(rest[pos + 1:])
    for r0 in range(0, o_ref.shape[0], sub_m):
        rs = slice(r0, r0 + sub_m)
        gated = []
        c0 = 0
        for g_ref in gate_refs:
            cs = slice(c0, c0 + g_ref.shape[1])
            gated.append((o_ref[rs, cs].astype(F32) * _silu(g_ref[rs, :].astype(F32))).astype(BF16))
            c0 = cs.stop
        assert c0 == o_ref.shape[1]
        y = jnp.dot(jnp.concatenate(gated, axis=1), w_ref[...], preferred_element_type=F32)
        h = res_ref[rs, :] + _rms_scale(y) * gain_ref[...]
        h_ref[rs, :] = h
        if layout:
            hn = _rms_scale(h)
        seg_refs = iter(out_refs)
        for refs, (_, segments) in zip(nexts, layout):
            u = (hn * refs[0][...]).astype(BF16)
            col = 0
            for width, head_width in segments:
                out_ref = next(seg_refs)
                for c0 in range(col, col + width, sub_n):
                    cs = slice(c0, c0 + sub_n)
                    acc = jnp.dot(u, refs[1][:, cs], preferred_element_type=F32)
                    if len(refs) == 3:
                        acc = acc * refs[2][:, cs]
                    if head_width:
                        for t in range(sub_n // head_width):
                            head = (c0 - col) // head_width + t
                            out_ref[head, rs, :] = acc[:, t * head_width:(t + 1) * head_width].astype(
                                out_ref.dtype)
                    else:
                        out_ref[rs, c0 - col:c0 - col + sub_n] = acc.astype(out_ref.dtype)
                col += width


def _out_proj(o, gate, w, gain, res, nexts, *, tm, sub_m, name, sub_n=512):
    n, kdim = o.shape
    d = w.shape[1]
    row_spec = lambda width, col=0: pl.BlockSpec((tm, width), lambda i: (i, col))
    whole = lambda a: pl.BlockSpec(a.shape, lambda i: (0, 0), pipeline_mode=pl.Buffered(1))
    gate_arr, gate_width, gate_block0 = gate
    n_gate = kdim // gate_width
    assert n_gate * gate_width == kdim
    operands = [o] + [gate_arr] * n_gate + [w, gain, res]
    in_specs = ([row_spec(kdim)] + [row_spec(gate_width, gate_block0 + t) for t in range(n_gate)]
                + [whole(w), whole(gain), row_spec(d)])
    layout = []
    out_shape = [jax.ShapeDtypeStruct((n, d), F32)]
    out_specs = [row_spec(d)]
    for gain_i, w_i, scale_i, segments in nexts:
        group = [gain_i, w_i] + ([] if scale_i is None else [scale_i])
        operands += group
        in_specs += [whole(a) for a in group]
        layout.append((scale_i is not None, tuple(segments)))
        assert sum(width for width, _ in segments) == w_i.shape[1]
        for width, head_width in segments:
            assert width % sub_n == 0 and (head_width == 0 or sub_n % head_width == 0)
            if head_width:
                heads = width // head_width
                out_shape.append(jax.ShapeDtypeStruct((heads, n, head_width), BF16))
                out_specs.append(pl.BlockSpec((heads, tm, head_width), lambda i: (0, i, 0)))
            else:
                out_shape.append(jax.ShapeDtypeStruct((n, width), BF16))
                out_specs.append(row_spec(width))
    return pl.pallas_call(
        functools.partial(_out_proj_kernel, n_gate=n_gate, layout=tuple(layout), sub_m=sub_m,
                          sub_n=sub_n),
        grid=(n // tm,),
        in_specs=in_specs,
        out_specs=out_specs,
        out_shape=out_shape,
        compiler_params=_params(("parallel",)),
        name=name,
    )(*operands)


def _sb_kernel(ph_ref, pq_ref, pj_ref, q_ref, k_ref, v_ref, mm_ref, o_ref,
               za0_ref, za1_ref, zb0_ref, zb1_ref, pba0_ref, pba1_ref, pbb0_ref, pbb1_ref,
               zpa0_ref, zpa1_ref, zpb0_ref, zpb1_ref, p0a0_ref, p0a1_ref, p0b0_ref, p0b1_ref,
               wa_ref, wb_ref, acc_ref, carry_ref, *, tq, tk, nq, n_masked, n_pairs):
    z_refs = ((za0_ref, za1_ref), (zb0_ref, zb1_ref))
    pb_refs = ((pba0_ref, pba1_ref), (pbb0_ref, pbb1_ref))
    p0_refs = ((p0a0_ref, p0a1_ref), (p0b0_ref, p0b1_ref))
    zp_refs = ((zpa0_ref, zpa1_ref), (zpb0_ref, zpb1_ref))
    w_refs = (wa_ref, wb_ref)

    def rows(qi):
        return pl.ds(pl.multiple_of(qi * tq, tq), tq)

    def live_rows(masked, e):
        return slice(tk if (masked and e == 0) else 0, tq)

    def slot(p):
        return ph_ref[p] * nq + pq_ref[p]

    def qk(p, par, masked):
        hd, qi, jh = ph_ref[p], pq_ref[p], pj_ref[p]
        q = q_ref[hd, rows(qi), :]
        for e in range(2):
            rs = live_rows(masked, e)
            kb = k_ref[hd, pl.ds(pl.multiple_of((jh - e) * tk, tk), tk), :]
            z_refs[par][e][rs, :] = lax.dot_general(q[rs, :], kb, (((1,), (1,)), ((), ())),
                                                    preferred_element_type=F32)

    def score(p, par, masked):
        for e in range(2):
            rs = live_rows(masked, e)
            z2 = z_refs[par][e][rs, :]
            t = jnp.log2(1.0 + jnp.exp2(-jnp.abs(z2)))
            pos = jnp.maximum(z2, 0.0) + t
            zp = z2 - pos
            if masked:
                causal = (lax.broadcasted_iota(jnp.int32, z2.shape, 1)
                          < lax.broadcasted_iota(jnp.int32, z2.shape, 0))
                pos = jnp.where(causal, pos, 0.0)
                zp = jnp.where(causal, zp, MASKED_LOG2_WEIGHT)
            pb_refs[par][e][rs, :] = pos.astype(BF16)
            p0_refs[par][e][rs, :] = pos[:, 0:1]
            zp_refs[par][e][rs, :] = zp

    def weights(p, par, first):
        a = slot(p)
        carry = None if first else carry_ref[a]
        for e in range(2):
            rs = live_rows(first, e)
            cols = slice((1 - e) * tk, (2 - e) * tk)
            sfx = jnp.dot(pb_refs[par][e][rs, :], mm_ref[...], preferred_element_type=F32)
            x = zp_refs[par][e][rs, :] - sfx
            if carry is not None:
                x = x - carry
            w_refs[par][rs, cols] = jnp.exp2(x).astype(BF16)
            row_sum = sfx[:, 0:1] + p0_refs[par][e][rs, :]
            if rs.start:
                w_refs[par][0:rs.start, cols] = jnp.zeros((rs.start, tk), BF16)
                row_sum = jnp.concatenate([jnp.zeros((rs.start, 1), F32), row_sum], axis=0)
            carry = row_sum if carry is None else carry + row_sum
        carry_ref[a] = carry

    def pv(p, par, first):
        a, jh = slot(p), pj_ref[p]
        vb = v_ref[ph_ref[p], pl.ds(pl.multiple_of((jh - 1) * tk, tk), 2 * tk), :]
        out = jnp.dot(w_refs[par][...], vb, preferred_element_type=F32)
        if first:
            acc_ref[a] = out
        else:
            acc_ref[a] += out

    def body(b, par, masked_of):
        if masked_of(2) is not None:
            weights(b - 2, par, first=masked_of(2))
        if masked_of(3) is not None:
            pv(b - 3, 1 - par, first=masked_of(3))
        if masked_of(0) is not None:
            qk(b, par, masked_of(0))
        if masked_of(1) is not None:
            score(b - 1, 1 - par, masked=masked_of(1))

    def static_flags(b):
        def masked_of(s):
            p = b - s
            return None if not 0 <= p < n_pairs else p < n_masked
        return masked_of

    def peel(lo, hi):
        for b in range(lo, hi):
            body(b, b % 2, static_flags(b))

    def loop(lo, hi, flag, unroll):
        assert (hi - lo) % unroll == 0 and unroll % 2 == 0

        def step(m, c):
            for d in range(unroll):
                body(lo + unroll * m + d, (lo + d) % 2, lambda s: flag)
            return c
        lax.fori_loop(0, (hi - lo) // unroll, step, 0)

    depth = 3
    m_unroll, u_unroll = 4, 8
    m_lo = depth + (n_masked - depth) % m_unroll
    u_lo = n_masked + depth + (n_pairs - n_masked - depth) % u_unroll
    assert m_lo <= n_masked and u_lo <= n_pairs
    peel(0, m_lo)
    loop(m_lo, n_masked, True, m_unroll)
    peel(n_masked, u_lo)
    loop(u_lo, n_pairs, False, u_unroll)
    peel(n_pairs, n_pairs + depth)

    dv = acc_ref.shape[2]
    for a in range(acc_ref.shape[0]):
        hd, qi = divmod(a, nq)
        o_ref[qi * tq:(qi + 1) * tq, hd * dv:(hd + 1) * dv] = acc_ref[a].astype(o_ref.dtype)


def _sb_attention(q_hm, k_hm, v_hm, *, batch, seq, tq=512, tk=256, heads_per_step=2):
    n_heads, n, dv = v_hm.shape
    assert q_hm.shape == k_hm.shape == (n_heads, n, SB_QK_DIM) and n == batch * seq
    assert seq % tq == 0 and tq == 2 * tk and n_heads % heads_per_step == 0
    nq = seq // tq
    hps = heads_per_step
    rr = lax.broadcasted_iota(jnp.int32, (tk, tk), 0)
    cc = lax.broadcasted_iota(jnp.int32, (tk, tk), 1)
    suffix_mat = (rr > cc).astype(BF16)
    pairs = [(hd, qi, 2 * qi + 1) for hd in range(hps) for qi in range(nq)]
    n_masked = len(pairs)
    pairs += [(hd, qi, jh) for hd in range(hps) for qi in range(nq)
              for jh in range(2 * qi - 1, 0, -2)]
    tables = [jnp.asarray(col, jnp.int32) for col in zip(*pairs)]
    kern = functools.partial(_sb_kernel, tq=tq, tk=tk, nq=nq, n_masked=n_masked,
                             n_pairs=len(pairs))
    stage_f32 = pltpu.VMEM((tq, tk), F32)
    stage_bf16 = pltpu.VMEM((tq, tk), BF16)
    stage_col = pltpu.VMEM((tq, 1), F32)
    head_block = lambda width: pl.BlockSpec((hps, seq, width), lambda b, g, *_: (g, b, 0))
    return pl.pallas_call(
        kern,
        grid_spec=pltpu.PrefetchScalarGridSpec(
            num_scalar_prefetch=len(tables),
            grid=(batch, n_heads // hps),
            in_specs=[head_block(SB_QK_DIM), head_block(SB_QK_DIM), head_block(dv),
                      pl.BlockSpec((tk, tk), lambda b, g, *_: (0, 0))],
            out_specs=pl.BlockSpec((seq, hps * dv), lambda b, g, *_: (b, g)),
            scratch_shapes=[stage_f32] * 4 + [stage_bf16] * 4 + [stage_f32] * 4 + [stage_col] * 4
            + [pltpu.VMEM((tq, 2 * tk), BF16)] * 2 + [
                pltpu.VMEM((hps * nq, tq, dv), F32), pltpu.VMEM((hps * nq, tq, 1), F32)],
        ),
        out_shape=jax.ShapeDtypeStruct((n, n_heads * dv), BF16),
        compiler_params=_params(("parallel", "parallel")),
        name="sb_attention",
    )(*tables, q_hm, k_hm, v_hm, suffix_mat)


def kernel(x, ret_norm_pre, ret_w_in, ret_w_out, ret_norm_post, kv_norm, w_kv,
           sb_norm_pre, sb_w_in, sb_w_out, sb_norm_post):
    batch, seq, d = x.shape
    assert ret_w_in.shape[0] == 1 and sb_w_in.shape[0] == 1
    n = batch * seq
    x2d = x.reshape(n, d)

    pos = jnp.arange(seq, dtype=F32)
    half = RET_QK_DIM // 2
    inv_freq = 1.0 / (ROPE_BASE ** (jnp.arange(half, dtype=F32) / half))
    ang = pos[:, None] * inv_freq[None, :]
    cos, sin = jnp.cos(ang), jnp.sin(ang)

    vec = lambda g: g.reshape(1, d).astype(F32)

    proj = _in_proj(x2d, vec(ret_norm_pre[0]), ret_w_in[0].astype(BF16), cos, sin, seq=seq)
    o = _retention(proj, _retention_tables(), batch=batch, seq=seq)
    qkw = SB_HEADS * SB_QK_DIM
    q_scale = math.log2(math.e) / math.sqrt(SB_QK_DIM)
    col_scale = jnp.ones((1, sb_w_in.shape[2]), F32).at[:, :qkw].set(q_scale)
    ret_width = RET_HEADS * RET_V_DIM
    sb_width = SB_HEADS * SB_V_DIM
    h1, k_hm, v_hm, q_hm, gate2 = _out_proj(
        o, (proj, ret_width, 2), ret_w_out[0].astype(BF16), vec(ret_norm_post[0]), x2d,
        [(vec(kv_norm), w_kv.astype(BF16), None, [(qkw, SB_QK_DIM), (sb_width, SB_V_DIM)]),
         (vec(sb_norm_pre[0]), sb_w_in[0].astype(BF16), col_scale, [(qkw, SB_QK_DIM), (sb_width, 0)])],
        tm=512, sub_m=512, name="ret_out_kv_q_proj")
    o2 = _sb_attention(q_hm, k_hm, v_hm, batch=batch, seq=seq)
    (out,) = _out_proj(o2, (gate2, sb_width, 0), sb_w_out[0].astype(BF16), vec(sb_norm_post[0]),
                       h1, [], tm=1024, sub_m=256, name="sb_out_proj")
    return out.reshape(batch, seq, d)
```

```python
import functools
import math

import jax
import jax.numpy as jnp
from jax import lax
from jax.experimental import pallas as pl
from jax.experimental.pallas import tpu as pltpu

EPS = 1e-6
ROPE_BASE = 10000.0

RET_HEADS = 4
RET_QK_DIM = 256
RET_V_DIM = 512
RET_CHUNK = 256

SB_HEADS = 8
SB_QK_DIM = 128
SB_V_DIM = 256

VMEM_LIMIT_BYTES = 56 * 1024 * 1024

MASKED_LOG2_WEIGHT = -1e30

F32 = jnp.float32
BF16 = jnp.bfloat16


def _params(semantics):
    return pltpu.CompilerParams(dimension_semantics=semantics,
                                vmem_limit_bytes=VMEM_LIMIT_BYTES)


def _rms_scale(x):
    return x * lax.rsqrt(jnp.mean(x * x, axis=-1, keepdims=True) + EPS)


def _silu(g):
    return g / (1.0 + jnp.exp2(g * (-math.log2(math.e))))


def _in_proj_kernel(x_ref, gain_ref, w_ref, cos_a_ref, sin_a_ref, cos_b_ref, sin_b_ref, dec_ref,
                    o_ref, *, n_rope_groups, pos_blocks, sub_m, sub_n):
    half = RET_QK_DIM // 2
    rope_cols = n_rope_groups * RET_QK_DIM
    blk = pl.program_id(0) % pos_blocks
    cos_a = cos_a_ref[pl.ds(blk, 1), :]
    sin_a = sin_a_ref[pl.ds(blk, 1), :]
    for r0 in range(0, o_ref.shape[0], sub_m):
        rs = slice(r0, r0 + sub_m)
        u = (_rms_scale(x_ref[rs, :]) * gain_ref[...]).astype(BF16)
        cos_b = cos_b_ref[rs, :]
        sin_b = sin_b_ref[rs, :]
        cos = cos_a * cos_b - sin_a * sin_b
        sin = sin_a * cos_b + cos_a * sin_b
        for grp in range(n_rope_groups):
            c0 = grp * RET_QK_DIM
            acc = jnp.dot(u, w_ref[:, c0:c0 + RET_QK_DIM], preferred_element_type=F32)
            dec = dec_ref[grp, rs, :]
            cos_g = cos * dec
            sin_g = sin * dec
            x1 = acc[:, :half]
            x2 = acc[:, half:]
            o_ref[rs, c0:c0 + half] = (x1 * cos_g - x2 * sin_g).astype(o_ref.dtype)
            o_ref[rs, c0 + half:c0 + RET_QK_DIM] = (x1 * sin_g + x2 * cos_g).astype(o_ref.dtype)
        for c0 in range(rope_cols, o_ref.shape[1], sub_n):
            o_ref[rs, c0:c0 + sub_n] = jnp.dot(
                u, w_ref[:, c0:c0 + sub_n], preferred_element_type=F32).astype(o_ref.dtype)


def _in_proj(x2d, gain, w, *, seq, tm=512, sub_m=256):
    n, d = x2d.shape
    n_out = w.shape[1]
    n_rope_groups = 2 * RET_HEADS
    half = RET_QK_DIM // 2
    assert n % tm == 0 and seq % tm == 0 and tm % RET_CHUNK == 0
    assert (n_out - n_rope_groups * RET_QK_DIM) % 512 == 0
    pos_blocks = seq // tm
    inv_freq = 1.0 / (ROPE_BASE ** (jnp.arange(half, dtype=F32) / half))
    ang_a = (jnp.arange(pos_blocks, dtype=F32) * tm)[:, None] * inv_freq[None, :]
    ang_b = jnp.arange(tm, dtype=F32)[:, None] * inv_freq[None, :]
    log_gamma = jnp.log1p(-jnp.exp2(-5.0 - jnp.arange(RET_HEADS, dtype=F32)))
    steps = (jnp.arange(tm, dtype=F32) % RET_CHUNK) + 1.0
    q_dec = jnp.exp(log_gamma[:, None] * steps[None, :])
    k_dec = jnp.exp(-log_gamma[:, None] * steps[None, :]) * (RET_QK_DIM ** -0.5)
    dec = jnp.broadcast_to(jnp.concatenate([q_dec, k_dec], axis=0)[:, :, None],
                           (n_rope_groups, tm, half))
    kern = functools.partial(_in_proj_kernel, n_rope_groups=n_rope_groups, pos_blocks=pos_blocks,
                             sub_m=sub_m, sub_n=512)
    whole = lambda a: pl.BlockSpec(a.shape, lambda i: (0,) * a.ndim, pipeline_mode=pl.Buffered(1))
    tables = [jnp.cos(ang_a), jnp.sin(ang_a), jnp.cos(ang_b), jnp.sin(ang_b), dec]
    return pl.pallas_call(
        kern,
        grid=(n // tm,),
        in_specs=[pl.BlockSpec((tm, d), lambda i: (i, 0)), whole(gain), whole(w)]
        + [whole(t) for t in tables],
        out_specs=pl.BlockSpec((tm, n_out), lambda i: (i, 0)),
        out_shape=jax.ShapeDtypeStruct((n, n_out), BF16),
        compiler_params=_params(("parallel",)),
        name="ret_in_proj",
    )(x2d, gain, w, *tables)


def _retention_kernel(q_ref, k_ref, v_ref, cdec_ref, o_ref, state_ref, *, chunks_per_step):
    t = pl.program_id(1)

    @pl.when(t == 0)
    def _():
        state_ref[...] = jnp.zeros_like(state_ref)

    c = RET_CHUNK
    causal = (lax.broadcasted_iota(jnp.int32, (c, c), 0)
              >= lax.broadcasted_iota(jnp.int32, (c, c), 1))
    for ci in range(chunks_per_step):
        r0 = ci * c
        for h in range(RET_HEADS):
            q = q_ref[r0:r0 + c, h * RET_QK_DIM:(h + 1) * RET_QK_DIM]
            k = k_ref[r0:r0 + c, h * RET_QK_DIM:(h + 1) * RET_QK_DIM]
            v = v_ref[r0:r0 + c, h * RET_V_DIM:(h + 1) * RET_V_DIM]
            state = state_ref[h]
            scores = lax.dot_general(q, k, (((1,), (1,)), ((), ())), preferred_element_type=F32)
            scores = jnp.where(causal, scores, 0.0)
            o = jnp.dot(jnp.concatenate([scores.astype(BF16), q], axis=1),
                        jnp.concatenate([v, state.astype(BF16)], axis=0),
                        preferred_element_type=F32)
            kv = lax.dot_general(k, v, (((0,), (0,)), ((), ())), preferred_element_type=F32)
            state_ref[h] = (state + kv) * cdec_ref[h]
            mu = jnp.mean(o, axis=-1, keepdims=True)
            d = o - mu
            var = jnp.mean(d * d, axis=-1, keepdims=True)
            on = d * lax.rsqrt(var + EPS)
            o_ref[r0:r0 + c, h * RET_V_DIM:(h + 1) * RET_V_DIM] = on.astype(o_ref.dtype)


def _retention(proj, *, batch, seq, rows=512):
    n = proj.shape[0]
    qw = RET_HEADS * RET_QK_DIM
    vw = RET_HEADS * RET_V_DIM
    assert proj.shape[1] == 2 * qw + 2 * vw and vw == 2 * qw
    assert seq % rows == 0 and rows % RET_CHUNK == 0
    steps = seq // rows
    log_gamma = jnp.log1p(-jnp.exp2(-5.0 - jnp.arange(RET_HEADS, dtype=F32)))
    cdec = jnp.broadcast_to(jnp.exp(log_gamma * RET_CHUNK)[:, None, None],
                            (RET_HEADS, 1, RET_V_DIM))
    kern = functools.partial(_retention_kernel, chunks_per_step=rows // RET_CHUNK)
    row = lambda b, t: b * steps + t
    return pl.pallas_call(
        kern,
        grid=(batch, steps),
        in_specs=[
            pl.BlockSpec((rows, qw), lambda b, t: (row(b, t), 0)),
            pl.BlockSpec((rows, qw), lambda b, t: (row(b, t), 1)),
            pl.BlockSpec((rows, vw), lambda b, t: (row(b, t), 1)),
            pl.BlockSpec(cdec.shape, lambda b, t: (0, 0, 0)),
        ],
        out_specs=pl.BlockSpec((rows, vw), lambda b, t: (row(b, t), 0)),
        out_shape=jax.ShapeDtypeStruct((n, vw), BF16),
        scratch_shapes=[pltpu.VMEM((RET_HEADS, RET_QK_DIM, RET_V_DIM), F32)],
        compiler_params=_params(("parallel", "arbitrary")),
        name="retention",
    )(proj, proj, proj, cdec)


def _out_proj_kernel(o_ref, *rest, n_gate, layout, sub_m, sub_n):
    gate_refs = rest[:n_gate]
    w_ref, gain_ref, res_ref = rest[n_gate:n_gate + 3]
    rest = rest[n_gate + 3:]
    nexts = []
    pos = 0
    for has_scale, _ in layout:
        width = 3 if has_scale else 2
        nexts.append(rest[pos:pos + width])
        pos += width
    h_ref = rest[pos]
    out_refs = list(rest[pos + 1:])
    for r0 in range(0, o_ref.shape[0], sub_m):
        rs = slice(r0, r0 + sub_m)
        gated = []
        c0 = 0
        for g_ref in gate_refs:
            cs = slice(c0, c0 + g_ref.shape[1])
            gated.append((o_ref[rs, cs].astype(F32) * _silu(g_ref[rs, :].astype(F32))).astype(BF16))
            c0 = cs.stop
        assert c0 == o_ref.shape[1]
        y = jnp.dot(jnp.concatenate(gated, axis=1), w_ref[...], preferred_element_type=F32)
        h = res_ref[rs, :] + _rms_scale(y) * gain_ref[...]
        h_ref[rs, :] = h
        if layout:
            hn = _rms_scale(h)
        seg_refs = iter(out_refs)
        for refs, (_, segments) in zip(nexts, layout):
            u = (hn * refs[0][...]).astype(BF16)
            col = 0
            for width, head_width in segments:
                out_ref = next(seg_refs)
                for c0 in range(col, col + width, sub_n):
                    cs = slice(c0, c0 + sub_n)
                    acc = jnp.dot(u, refs[1][:, cs], preferred_element_type=F32)
                    if len(refs) == 3:
                        acc = acc * refs[2][:, cs]
                    if head_width:
                        for t in range(sub_n // head_width):
                            head = (c0 - col) // head_width + t
                            out_ref[head, rs, :] = acc[:, t * head_width:(t + 1) * head_width].astype(
                                out_ref.dtype)
                    else:
                        out_ref[rs, c0 - col:c0 - col + sub_n] = acc.astype(out_ref.dtype)
                col += width


def _out_proj(o, gate, w, gain, res, nexts, *, tm, sub_m, name, sub_n=512):
    n, kdim = o.shape
    d = w.shape[1]
    row_spec = lambda width, col=0: pl.BlockSpec((tm, width), lambda i: (i, col))
    whole = lambda a: pl.BlockSpec(a.shape, lambda i: (0, 0), pipeline_mode=pl.Buffered(1))
    gate_arr, gate_width, gate_block0 = gate
    n_gate = kdim // gate_width
    assert n_gate * gate_width == kdim
    operands = [o] + [gate_arr] * n_gate + [w, gain, res]
    in_specs = ([row_spec(kdim)] + [row_spec(gate_width, gate_block0 + t) for t in range(n_gate)]
                + [whole(w), whole(gain), row_spec(d)])
    layout = []
    out_shape = [jax.ShapeDtypeStruct((n, d), F32)]
    out_specs = [row_spec(d)]
    for gain_i, w_i, scale_i, segments in nexts:
        group = [gain_i, w_i] + ([] if scale_i is None else [scale_i])
        operands += group
        in_specs += [whole(a) for a in group]
        layout.append((scale_i is not None, tuple(segments)))
        assert sum(width for width, _ in segments) == w_i.shape[1]
        for width, head_width in segments:
            assert width % sub_n == 0 and (head_width == 0 or sub_n % head_width == 0)
            if head_width:
                heads = width // head_width
                out_shape.append(jax.ShapeDtypeStruct((heads, n, head_width), BF16))
                out_specs.append(pl.BlockSpec((heads, tm, head_width), lambda i: (0, i, 0)))
            else:
                out_shape.append(jax.ShapeDtypeStruct((n, width), BF16))
                out_specs.append(row_spec(width))
    return pl.pallas_call(
        functools.partial(_out_proj_kernel, n_gate=n_gate, layout=tuple(layout), sub_m=sub_m,
                          sub_n=sub_n),
        grid=(n // tm,),
        in_specs=in_specs,
        out_specs=out_specs,
        out_shape=out_shape,
        compiler_params=_params(("parallel",)),
        name=name,
    )(*operands)


def _sb_kernel(ph_ref, pq_ref, pj_ref, q_ref, k_ref, v_ref, mm_ref, o_ref,
               za0_ref, za1_ref, zb0_ref, zb1_ref, pba0_ref, pba1_ref, pbb0_ref, pbb1_ref,
               zpa0_ref, zpa1_ref, zpb0_ref, zpb1_ref, p0a0_ref, p0a1_ref, p0b0_ref, p0b1_ref,
               wa_ref, wb_ref, acc_ref, carry_ref, *, tq, tk, nq, n_masked, n_pairs):
    z_refs = ((za0_ref, za1_ref), (zb0_ref, zb1_ref))
    pb_refs = ((pba0_ref, pba1_ref), (pbb0_ref, pbb1_ref))
    p0_refs = ((p0a0_ref, p0a1_ref), (p0b0_ref, p0b1_ref))
    zp_refs = ((zpa0_ref, zpa1_ref), (zpb0_ref, zpb1_ref))
    w_refs = (wa_ref, wb_ref)

    def rows(qi):
        return pl.ds(pl.multiple_of(qi * tq, tq), tq)

    def live_rows(masked, e):
        return slice(tk if (masked and e == 0) else 0, tq)

    def slot(p):
        return ph_ref[p] * nq + pq_ref[p]

    def qk(p, par, masked):
        hd, qi, jh = ph_ref[p], pq_ref[p], pj_ref[p]
        q = q_ref[hd, rows(qi), :]
        for e in range(2):
            rs = live_rows(masked, e)
            kb = k_ref[hd, pl.ds(pl.multiple_of((jh - e) * tk, tk), tk), :]
            z_refs[par][e][rs, :] = lax.dot_general(q[rs, :], kb, (((1,), (1,)), ((), ())),
                                                    preferred_element_type=F32)

    def score(p, par, masked):
        for e in range(2):
            rs = live_rows(masked, e)
            z2 = z_refs[par][e][rs, :]
            t = jnp.log2(1.0 + jnp.exp2(-jnp.abs(z2)))
            pos = jnp.maximum(z2, 0.0) + t
            zp = z2 - pos
            if masked:
                causal = (lax.broadcasted_iota(jnp.int32, z2.shape, 1)
                          < lax.broadcasted_iota(jnp.int32, z2.shape, 0))
                pos = jnp.where(causal, pos, 0.0)
                zp = jnp.where(causal, zp, MASKED_LOG2_WEIGHT)
            pb_refs[par][e][rs, :] = pos.astype(BF16)
            p0_refs[par][e][rs, :] = pos[:, 0:1]
            zp_refs[par][e][rs, :] = zp

    def weights(p, par, first):
        a = slot(p)
        carry = None if first else carry_ref[a]
        for e in range(2):
            rs = live_rows(first, e)
            cols = slice((1 - e) * tk, (2 - e) * tk)
            sfx = jnp.dot(pb_refs[par][e][rs, :], mm_ref[...], preferred_element_type=F32)
            x = zp_refs[par][e][rs, :] - sfx
            if carry is not None:
                x = x - carry
            w_refs[par][rs, cols] = jnp.exp2(x).astype(BF16)
            row_sum = sfx[:, 0:1] + p0_refs[par][e][rs, :]
            if rs.start:
                w_refs[par][0:rs.start, cols] = jnp.zeros((rs.start, tk), BF16)
                row_sum = jnp.concatenate([jnp.zeros((rs.start, 1), F32), row_sum], axis=0)
            carry = row_sum if carry is None else carry + row_sum
        carry_ref[a] = carry

    def pv(p, par, first):
        a, jh = slot(p), pj_ref[p]
        vb = v_ref[ph_ref[p], pl.ds(pl.multiple_of((jh - 1) * tk, tk), 2 * tk), :]
        out = jnp.dot(w_refs[par][...], vb, preferred_element_type=F32)
        if first:
            acc_ref[a] = out
        else:
            acc_ref[a] += out

    def body(b, par, masked_of):
        if masked_of(2) is not None:
            weights(b - 2, par, first=masked_of(2))
        if masked_of(3) is not None:
            pv(b - 3, 1 - par, first=masked_of(3))
        if masked_of(0) is not None:
            qk(b, par, masked_of(0))
        if masked_of(1) is not None:
            score(b - 1, 1 - par, masked=masked_of(1))

    def static_flags(b):
        def masked_of(s):
            p = b - s
            return None if not 0 <= p < n_pairs else p < n_masked
        return masked_of

    def peel(lo, hi):
        for b in range(lo, hi):
            body(b, b % 2, static_flags(b))

    def loop(lo, hi, flag, unroll):
        assert (hi - lo) % unroll == 0 and unroll % 2 == 0

        def step(m, c):
            for d in range(unroll):
                body(lo + unroll * m + d, (lo + d) % 2, lambda s: flag)
            return c
        lax.fori_loop(0, (hi - lo) // unroll, step, 0)

    depth = 3
    m_unroll, u_unroll = 4, 8
    m_lo = depth + (n_masked - depth) % m_unroll
    u_lo = n_masked + depth + (n_pairs - n_masked - depth) % u_unroll
    assert m_lo <= n_masked and u_lo <= n_pairs
    peel(0, m_lo)
    loop(m_lo, n_masked, True, m_unroll)
    peel(n_masked, u_lo)
    loop(u_lo, n_pairs, False, u_unroll)
    peel(n_pairs, n_pairs + depth)

    dv = acc_ref.shape[2]
    for a in range(acc_ref.shape[0]):
        hd, qi = divmod(a, nq)
        o_ref[qi * tq:(qi + 1) * tq, hd * dv:(hd + 1) * dv] = acc_ref[a].astype(o_ref.dtype)


def _sb_attention(q_hm, k_hm, v_hm, *, batch, seq, tq=512, tk=256, heads_per_step=2):
    n_heads, n, dv = v_hm.shape
    assert q_hm.shape == k_hm.shape == (n_heads, n, SB_QK_DIM) and n == batch * seq
    assert seq % tq == 0 and tq == 2 * tk and n_heads % heads_per_step == 0
    nq = seq // tq
    hps = heads_per_step
    rr = lax.broadcasted_iota(jnp.int32, (tk, tk), 0)
    cc = lax.broadcasted_iota(jnp.int32, (tk, tk), 1)
    suffix_mat = (rr > cc).astype(BF16)
    pairs = [(hd, qi, 2 * qi + 1) for hd in range(hps) for qi in range(nq)]
    n_masked = len(pairs)
    pairs += [(hd, qi, jh) for hd in range(hps) for qi in range(nq)
              for jh in range(2 * qi - 1, 0, -2)]
    tables = [jnp.asarray(col, jnp.int32) for col in zip(*pairs)]
    kern = functools.partial(_sb_kernel, tq=tq, tk=tk, nq=nq, n_masked=n_masked,
                             n_pairs=len(pairs))
    stage_f32 = pltpu.VMEM((tq, tk), F32)
    stage_bf16 = pltpu.VMEM((tq, tk), BF16)
    stage_col = pltpu.VMEM((tq, 1), F32)
    head_block = lambda width: pl.BlockSpec((hps, seq, width), lambda b, g, *_: (g, b, 0))
    return pl.pallas_call(
        kern,
        grid_spec=pltpu.PrefetchScalarGridSpec(
            num_scalar_prefetch=len(tables),
            grid=(batch, n_heads // hps),
            in_specs=[head_block(SB_QK_DIM), head_block(SB_QK_DIM), head_block(dv),
                      pl.BlockSpec((tk, tk), lambda b, g, *_: (0, 0))],
            out_specs=pl.BlockSpec((seq, hps * dv), lambda b, g, *_: (b, g)),
            scratch_shapes=[stage_f32] * 4 + [stage_bf16] * 4 + [stage_f32] * 4 + [stage_col] * 4
            + [pltpu.VMEM((tq, 2 * tk), BF16)] * 2 + [
                pltpu.VMEM((hps * nq, tq, dv), F32), pltpu.VMEM((hps * nq, tq, 1), F32)],
        ),
        out_shape=jax.ShapeDtypeStruct((n, n_heads * dv), BF16),
        compiler_params=_params(("parallel", "parallel")),
        name="sb_attention",
    )(*tables, q_hm, k_hm, v_hm, suffix_mat)


def kernel(x, ret_norm_pre, ret_w_in, ret_w_out, ret_norm_post, kv_norm, w_kv,
           sb_norm_pre, sb_w_in, sb_w_out, sb_norm_post):
    batch, seq, d = x.shape
    assert ret_w_in.shape[0] == 1 and sb_w_in.shape[0] == 1
    n = batch * seq
    x2d = x.reshape(n, d)

    vec = lambda g: g.reshape(1, d).astype(F32)

    proj = _in_proj(x2d, vec(ret_norm_pre[0]), ret_w_in[0].astype(BF16), seq=seq)
    o = _retention(proj, batch=batch, seq=seq)
    qkw = SB_HEADS * SB_QK_DIM
    q_scale = math.log2(math.e) / math.sqrt(SB_QK_DIM)
    col_scale = jnp.ones((1, sb_w_in.shape[2]), F32).at[:, :qkw].set(q_scale)
    ret_width = RET_HEADS * RET_V_DIM
    sb_width = SB_HEADS * SB_V_DIM
    h1, k_hm, v_hm, q_hm, gate2 = _out_proj(
        o, (proj, ret_width, 2), ret_w_out[0].astype(BF16), vec(ret_norm_post[0]), x2d,
        [(vec(kv_norm), w_kv.astype(BF16), None, [(qkw, SB_QK_DIM), (sb_width, SB_V_DIM)]),
         (vec(sb_norm_pre[0]), sb_w_in[0].astype(BF16), col_scale, [(qkw, SB_QK_DIM), (sb_width, 0)])],
        tm=512, sub_m=512, name="ret_out_kv_q_proj")
    o2 = _sb_attention(q_hm, k_hm, v_hm, batch=batch, seq=seq)
    (out,) = _out_proj(o2, (gate2, sb_width, 0), sb_w_out[0].astype(BF16), vec(sb_norm_post[0]),
                       h1, [], tm=1024, sub_m=256, name="sb_out_proj")
    return out.reshape(batch, seq, d)
```

```python
import functools
import math

import jax
import jax.numpy as jnp
from jax import lax
from jax.experimental import pallas as pl
from jax.experimental.pallas import tpu as pltpu

EPS = 1e-6
ROPE_BASE = 10000.0

RET_HEADS = 4
RET_QK_DIM = 256
RET_V_DIM = 512
RET_CHUNK = 256

SB_HEADS = 8
SB_QK_DIM = 128
SB_V_DIM = 256

VMEM_LIMIT_BYTES = 56 * 1024 * 1024

MASKED_LOG2_WEIGHT = -1e30

F32 = jnp.float32
BF16 = jnp.bfloat16


def _params(semantics):
    return pltpu.CompilerParams(dimension_semantics=semantics,
                                vmem_limit_bytes=VMEM_LIMIT_BYTES)


def _rms_scale(x):
    return x * lax.rsqrt(jnp.mean(x * x, axis=-1, keepdims=True) + EPS)


def _silu(g):
    return g / (1.0 + jnp.exp2(g * (-math.log2(math.e))))


def _in_proj_kernel(x_ref, gain_ref, w_ref, cos_a_ref, sin_a_ref, cos_b_ref, sin_b_ref, dec_ref,
                    o_ref, *, n_rope_groups, pos_blocks, sub_m, sub_n):
    half = RET_QK_DIM // 2
    rope_cols = n_rope_groups * RET_QK_DIM
    blk = pl.program_id(0) % pos_blocks
    cos_a = cos_a_ref[pl.ds(blk, 1), :]
    sin_a = sin_a_ref[pl.ds(blk, 1), :]
    for r0 in range(0, o_ref.shape[0], sub_m):
        rs = slice(r0, r0 + sub_m)
        u = (_rms_scale(x_ref[rs, :]) * gain_ref[...]).astype(BF16)
        cos_b = cos_b_ref[rs, :]
        sin_b = sin_b_ref[rs, :]
        cos = cos_a * cos_b - sin_a * sin_b
        sin = sin_a * cos_b + cos_a * sin_b
        for grp in range(n_rope_groups):
            c0 = grp * RET_QK_DIM
            acc = jnp.dot(u, w_ref[:, c0:c0 + RET_QK_DIM], preferred_element_type=F32)
            dec = dec_ref[grp, rs, :]
            cos_g = cos * dec
            sin_g = sin * dec
            x1 = acc[:, :half]
            x2 = acc[:, half:]
            o_ref[rs, c0:c0 + half] = (x1 * cos_g - x2 * sin_g).astype(o_ref.dtype)
            o_ref[rs, c0 + half:c0 + RET_QK_DIM] = (x1 * sin_g + x2 * cos_g).astype(o_ref.dtype)
        for c0 in range(rope_cols, o_ref.shape[1], sub_n):
            o_ref[rs, c0:c0 + sub_n] = jnp.dot(
                u, w_ref[:, c0:c0 + sub_n], preferred_element_type=F32).astype(o_ref.dtype)


def _in_proj(x2d, gain, w, *, seq, tm=512, sub_m=256):
    n, d = x2d.shape
    n_out = w.shape[1]
    n_rope_groups = 2 * RET_HEADS
    half = RET_QK_DIM // 2
    assert n % tm == 0 and seq % tm == 0 and tm % RET_CHUNK == 0
    assert (n_out - n_rope_groups * RET_QK_DIM) % 512 == 0
    pos_blocks = seq // tm
    inv_freq = 1.0 / (ROPE_BASE ** (jnp.arange(half, dtype=F32) / half))
    ang_a = (jnp.arange(pos_blocks, dtype=F32) * tm)[:, None] * inv_freq[None, :]
    ang_b = jnp.arange(tm, dtype=F32)[:, None] * inv_freq[None, :]
    log_gamma = jnp.log1p(-jnp.exp2(-5.0 - jnp.arange(RET_HEADS, dtype=F32)))
    steps = (jnp.arange(tm, dtype=F32) % RET_CHUNK) + 1.0
    q_dec = jnp.exp(log_gamma[:, None] * steps[None, :])
    k_dec = jnp.exp(-log_gamma[:, None] * steps[None, :]) * (RET_QK_DIM ** -0.5)
    dec = jnp.broadcast_to(jnp.concatenate([q_dec, k_dec], axis=0)[:, :, None],
                           (n_rope_groups, tm, half))
    kern = functools.partial(_in_proj_kernel, n_rope_groups=n_rope_groups, pos_blocks=pos_blocks,
                             sub_m=sub_m, sub_n=512)
    whole = lambda a: pl.BlockSpec(a.shape, lambda i: (0,) * a.ndim, pipeline_mode=pl.Buffered(1))
    tables = [jnp.cos(ang_a), jnp.sin(ang_a), jnp.cos(ang_b), jnp.sin(ang_b), dec]
    return pl.pallas_call(
        kern,
        grid=(n // tm,),
        in_specs=[pl.BlockSpec((tm, d), lambda i: (i, 0)), whole(gain), whole(w)]
        + [whole(t) for t in tables],
        out_specs=pl.BlockSpec((tm, n_out), lambda i: (i, 0)),
        out_shape=jax.ShapeDtypeStruct((n, n_out), BF16),
        compiler_params=_params(("parallel",)),
        name="ret_in_proj",
    )(x2d, gain, w, *tables)


def _retention_kernel(q_ref, k_ref, v_ref, cdec_ref, o_ref, state_ref, *, chunks_per_step):
    t = pl.program_id(1)

    @pl.when(t == 0)
    def _():
        state_ref[...] = jnp.zeros_like(state_ref)

    c = RET_CHUNK
    causal = (lax.broadcasted_iota(jnp.int32, (c, c), 0)
              >= lax.broadcasted_iota(jnp.int32, (c, c), 1))
    for ci in range(chunks_per_step):
        r0 = ci * c
        for h in range(RET_HEADS):
            q = q_ref[r0:r0 + c, h * RET_QK_DIM:(h + 1) * RET_QK_DIM]
            k = k_ref[r0:r0 + c, h * RET_QK_DIM:(h + 1) * RET_QK_DIM]
            v = v_ref[r0:r0 + c, h * RET_V_DIM:(h + 1) * RET_V_DIM]
            state = state_ref[h]
            scores = lax.dot_general(q, k, (((1,), (1,)), ((), ())), preferred_element_type=F32)
            scores = jnp.where(causal, scores, 0.0)
            o = jnp.dot(jnp.concatenate([scores.astype(BF16), q], axis=1),
                        jnp.concatenate([v, state.astype(BF16)], axis=0),
                        preferred_element_type=F32)
            kv = lax.dot_general(k, v, (((0,), (0,)), ((), ())), preferred_element_type=F32)
            state_ref[h] = (state + kv) * cdec_ref[h]
            mu = jnp.mean(o, axis=-1, keepdims=True)
            d = o - mu
            var = jnp.mean(d * d, axis=-1, keepdims=True)
            on = d * lax.rsqrt(var + EPS)
            o_ref[r0:r0 + c, h * RET_V_DIM:(h + 1) * RET_V_DIM] = on.astype(o_ref.dtype)


def _retention(proj, *, batch, seq, rows=1024):
    n = proj.shape[0]
    qw = RET_HEADS * RET_QK_DIM
    vw = RET_HEADS * RET_V_DIM
    assert proj.shape[1] == 2 * qw + 2 * vw and vw == 2 * qw
    assert seq % rows == 0 and rows % RET_CHUNK == 0
    steps = seq // rows
    log_gamma = jnp.log1p(-jnp.exp2(-5.0 - jnp.arange(RET_HEADS, dtype=F32)))
    cdec = jnp.broadcast_to(jnp.exp(log_gamma * RET_CHUNK)[:, None, None],
                            (RET_HEADS, 1, RET_V_DIM))
    kern = functools.partial(_retention_kernel, chunks_per_step=rows // RET_CHUNK)
    row = lambda b, t: b * steps + t
    return pl.pallas_call(
        kern,
        grid=(batch, steps),
        in_specs=[
            pl.BlockSpec((rows, qw), lambda b, t: (row(b, t), 0)),
            pl.BlockSpec((rows, qw), lambda b, t: (row(b, t), 1)),
            pl.BlockSpec((rows, vw), lambda b, t: (row(b, t), 1)),
            pl.BlockSpec(cdec.shape, lambda b, t: (0, 0, 0)),
        ],
        out_specs=pl.BlockSpec((rows, vw), lambda b, t: (row(b, t), 0)),
        out_shape=jax.ShapeDtypeStruct((n, vw), BF16),
        scratch_shapes=[pltpu.VMEM((RET_HEADS, RET_QK_DIM, RET_V_DIM), F32)],
        compiler_params=_params(("parallel", "arbitrary")),
        name="retention",
    )(proj, proj, proj, cdec)


def _out_proj_kernel(o_ref, *rest, n_gate, layout, sub_m, sub_n):
    gate_refs = rest[:n_gate]
    w_ref, gain_ref, res_ref = rest[n_gate:n_gate + 3]
    rest = rest[n_gate + 3:]
    nexts = []
    pos = 0
    for has_scale, _ in layout:
        width = 3 if has_scale else 2
        nexts.append(rest[pos:pos + width])
        pos += width
    h_ref = rest[pos]
    out_refs = list(rest[pos + 1:])
    for r0 in range(0, o_ref.shape[0], sub_m):
        rs = slice(r0, r0 + sub_m)
        gated = []
        c0 = 0
        for g_ref in gate_refs:
            cs = slice(c0, c0 + g_ref.shape[1])
            gated.append((o_ref[rs, cs].astype(F32) * _silu(g_ref[rs, :].astype(F32))).astype(BF16))
            c0 = cs.stop
        assert c0 == o_ref.shape[1]
        y = jnp.dot(jnp.concatenate(gated, axis=1), w_ref[...], preferred_element_type=F32)
        h = res_ref[rs, :] + _rms_scale(y) * gain_ref[...]
        h_ref[rs, :] = h
        if layout:
            hn = _rms_scale(h)
        seg_refs = iter(out_refs)
        for refs, (_, segments) in zip(nexts, layout):
            u = (hn * refs[0][...]).astype(BF16)
            col = 0
            for width, head_width in segments:
                out_ref = next(seg_refs)
                for c0 in range(col, col + width, sub_n):
                    cs = slice(c0, c0 + sub_n)
                    acc = jnp.dot(u, refs[1][:, cs], preferred_element_type=F32)
                    if len(refs) == 3:
                        acc = acc * refs[2][:, cs]
                    if head_width:
                        for t in range(sub_n // head_width):
                            head = (c0 - col) // head_width + t
                            out_ref[head, rs, :] = acc[:, t * head_width:(t + 1) * head_width].astype(
                                out_ref.dtype)
                    else:
                        out_ref[rs, c0 - col:c0 - col + sub_n] = acc.astype(out_ref.dtype)
                col += width


def _out_proj(o, gate, w, gain, res, nexts, *, tm, sub_m, name, sub_n=512):
    n, kdim = o.shape
    d = w.shape[1]
    row_spec = lambda width, col=0: pl.BlockSpec((tm, width), lambda i: (i, col))
    whole = lambda a: pl.BlockSpec(a.shape, lambda i: (0, 0), pipeline_mode=pl.Buffered(1))
    gate_arr, gate_width, gate_block0 = gate
    n_gate = kdim // gate_width
    assert n_gate * gate_width == kdim
    operands = [o] + [gate_arr] * n_gate + [w, gain, res]
    in_specs = ([row_spec(kdim)] + [row_spec(gate_width, gate_block0 + t) for t in range(n_gate)]
                + [whole(w), whole(gain), row_spec(d)])
    layout = []
    out_shape = [jax.ShapeDtypeStruct((n, d), F32)]
    out_specs = [row_spec(d)]
    for gain_i, w_i, scale_i, segments in nexts:
        group = [gain_i, w_i] + ([] if scale_i is None else [scale_i])
        operands += group
        in_specs += [whole(a) for a in group]
        layout.append((scale_i is not None, tuple(segments)))
        assert sum(width for width, _ in segments) == w_i.shape[1]
        for width, head_width in segments:
            assert width % sub_n == 0 and (head_width == 0 or sub_n % head_width == 0)
            if head_width:
                heads = width // head_width
                out_shape.append(jax.ShapeDtypeStruct((heads, n, head_width), BF16))
                out_specs.append(pl.BlockSpec((heads, tm, head_width), lambda i: (0, i, 0)))
            else:
                out_shape.append(jax.ShapeDtypeStruct((n, width), BF16))
                out_specs.append(row_spec(width))
    return pl.pallas_call(
        functools.partial(_out_proj_kernel, n_gate=n_gate, layout=tuple(layout), sub_m=sub_m,
                          sub_n=sub_n),
        grid=(n // tm,),
        in_specs=in_specs,
        out_specs=out_specs,
        out_shape=out_shape,
        compiler_params=_params(("parallel",)),
        name=name,
    )(*operands)


def _sb_kernel(ph_ref, pq_ref, pj_ref, q_ref, k_ref, v_ref, mm_ref, o_ref,
               za0_ref, za1_ref, zb0_ref, zb1_ref, pba0_ref, pba1_ref, pbb0_ref, pbb1_ref,
               zpa0_ref, zpa1_ref, zpb0_ref, zpb1_ref, p0a0_ref, p0a1_ref, p0b0_ref, p0b1_ref,
               wa_ref, wb_ref, acc_ref, carry_ref, *, tq, tk, nq, n_masked, n_pairs):
    z_refs = ((za0_ref, za1_ref), (zb0_ref, zb1_ref))
    pb_refs = ((pba0_ref, pba1_ref), (pbb0_ref, pbb1_ref))
    p0_refs = ((p0a0_ref, p0a1_ref), (p0b0_ref, p0b1_ref))
    zp_refs = ((zpa0_ref, zpa1_ref), (zpb0_ref, zpb1_ref))
    w_refs = (wa_ref, wb_ref)

    def rows(qi):
        return pl.ds(pl.multiple_of(qi * tq, tq), tq)

    def live_rows(masked, e):
        return slice(tk if (masked and e == 0) else 0, tq)

    def slot(p):
        return ph_ref[p] * nq + pq_ref[p]

    def qk(p, par, masked):
        hd, qi, jh = ph_ref[p], pq_ref[p], pj_ref[p]
        q = q_ref[hd, rows(qi), :]
        for e in range(2):
            rs = live_rows(masked, e)
            kb = k_ref[hd, pl.ds(pl.multiple_of((jh - e) * tk, tk), tk), :]
            z_refs[par][e][rs, :] = lax.dot_general(q[rs, :], kb, (((1,), (1,)), ((), ())),
                                                    preferred_element_type=F32)

    def score(p, par, masked):
        for e in range(2):
            rs = live_rows(masked, e)
            z2 = z_refs[par][e][rs, :]
            t = jnp.log2(1.0 + jnp.exp2(-jnp.abs(z2)))
            pos = jnp.maximum(z2, 0.0) + t
            zp = z2 - pos
            if masked:
                causal = (lax.broadcasted_iota(jnp.int32, z2.shape, 1)
                          < lax.broadcasted_iota(jnp.int32, z2.shape, 0))
                pos = jnp.where(causal, pos, 0.0)
                zp = jnp.where(causal, zp, MASKED_LOG2_WEIGHT)
            pb_refs[par][e][rs, :] = pos.astype(BF16)
            p0_refs[par][e][rs, :] = pos[:, 0:1]
            zp_refs[par][e][rs, :] = zp

    def weights(p, par, first):
        a = slot(p)
        carry = None if first else carry_ref[a]
        for e in range(2):
            rs = live_rows(first, e)
            cols = slice((1 - e) * tk, (2 - e) * tk)
            sfx = jnp.dot(pb_refs[par][e][rs, :], mm_ref[...], preferred_element_type=F32)
            x = zp_refs[par][e][rs, :] - sfx
            if carry is not None:
                x = x - carry
            w_refs[par][rs, cols] = jnp.exp2(x).astype(BF16)
            row_sum = sfx[:, 0:1] + p0_refs[par][e][rs, :]
            if rs.start:
                w_refs[par][0:rs.start, cols] = jnp.zeros((rs.start, tk), BF16)
                row_sum = jnp.concatenate([jnp.zeros((rs.start, 1), F32), row_sum], axis=0)
            carry = row_sum if carry is None else carry + row_sum
        carry_ref[a] = carry

    def pv(p, par, first):
        a, jh = slot(p), pj_ref[p]
        vb = v_ref[ph_ref[p], pl.ds(pl.multiple_of((jh - 1) * tk, tk), 2 * tk), :]
        out = jnp.dot(w_refs[par][...], vb, preferred_element_type=F32)
        if first:
            acc_ref[a] = out
        else:
            acc_ref[a] += out

    def body(b, par, masked_of):
        if masked_of(2) is not None:
            weights(b - 2, par, first=masked_of(2))
        if masked_of(3) is not None:
            pv(b - 3, 1 - par, first=masked_of(3))
        if masked_of(0) is not None:
            qk(b, par, masked_of(0))
        if masked_of(1) is not None:
            score(b - 1, 1 - par, masked=masked_of(1))

    def static_flags(b):
        def masked_of(s):
            p = b - s
            return None if not 0 <= p < n_pairs else p < n_masked
        return masked_of

    def peel(lo, hi):
        for b in range(lo, hi):
            body(b, b % 2, static_flags(b))

    def loop(lo, hi, flag, unroll):
        assert (hi - lo) % unroll == 0 and unroll % 2 == 0

        def step(m, c):
            for d in range(unroll):
                body(lo + unroll * m + d, (lo + d) % 2, lambda s: flag)
            return c
        lax.fori_loop(0, (hi - lo) // unroll, step, 0)

    depth = 3
    m_unroll, u_unroll = 4, 8
    m_lo = depth + (n_masked - depth) % m_unroll
    u_lo = n_masked + depth + (n_pairs - n_masked - depth) % u_unroll
    assert m_lo <= n_masked and u_lo <= n_pairs
    peel(0, m_lo)
    loop(m_lo, n_masked, True, m_unroll)
    peel(n_masked, u_lo)
    loop(u_lo, n_pairs, False, u_unroll)
    peel(n_pairs, n_pairs + depth)

    dv = acc_ref.shape[2]
    for a in range(acc_ref.shape[0]):
        hd, qi = divmod(a, nq)
        o_ref[qi * tq:(qi + 1) * tq, hd * dv:(hd + 1) * dv] = acc_ref[a].astype(o_ref.dtype)


def _sb_attention(q_hm, k_hm, v_hm, *, batch, seq, tq=512, tk=256, heads_per_step=2):
    n_heads, n, dv = v_hm.shape
    assert q_hm.shape == k_hm.shape == (n_heads, n, SB_QK_DIM) and n == batch * seq
    assert seq % tq == 0 and tq == 2 * tk and n_heads % heads_per_step == 0
    nq = seq // tq
    hps = heads_per_step
    rr = lax.broadcasted_iota(jnp.int32, (tk, tk), 0)
    cc = lax.broadcasted_iota(jnp.int32, (tk, tk), 1)
    suffix_mat = (rr > cc).astype(BF16)
    pairs = [(hd, qi, 2 * qi + 1) for hd in range(hps) for qi in range(nq)]
    n_masked = len(pairs)
    pairs += [(hd, qi, jh) for hd in range(hps) for qi in range(nq)
              for jh in range(2 * qi - 1, 0, -2)]
    tables = [jnp.asarray(col, jnp.int32) for col in zip(*pairs)]
    kern = functools.partial(_sb_kernel, tq=tq, tk=tk, nq=nq, n_masked=n_masked,
                             n_pairs=len(pairs))
    stage_f32 = pltpu.VMEM((tq, tk), F32)
    stage_bf16 = pltpu.VMEM((tq, tk), BF16)
    stage_col = pltpu.VMEM((tq, 1), F32)
    head_block = lambda width: pl.BlockSpec((hps, seq, width), lambda b, g, *_: (g, b, 0))
    return pl.pallas_call(
        kern,
        grid_spec=pltpu.PrefetchScalarGridSpec(
            num_scalar_prefetch=len(tables),
            grid=(batch, n_heads // hps),
            in_specs=[head_block(SB_QK_DIM), head_block(SB_QK_DIM), head_block(dv),
                      pl.BlockSpec((tk, tk), lambda b, g, *_: (0, 0))],
            out_specs=pl.BlockSpec((seq, hps * dv), lambda b, g, *_: (b, g)),
            scratch_shapes=[stage_f32] * 4 + [stage_bf16] * 4 + [stage_f32] * 4 + [stage_col] * 4
            + [pltpu.VMEM((tq, 2 * tk), BF16)] * 2 + [
                pltpu.VMEM((hps * nq, tq, dv), F32), pltpu.VMEM((hps * nq, tq, 1), F32)],
        ),
        out_shape=jax.ShapeDtypeStruct((n, n_heads * dv), BF16),
        compiler_params=_params(("parallel", "parallel")),
        name="sb_attention",
    )(*tables, q_hm, k_hm, v_hm, suffix_mat)


def kernel(x, ret_norm_pre, ret_w_in, ret_w_out, ret_norm_post, kv_norm, w_kv,
           sb_norm_pre, sb_w_in, sb_w_out, sb_norm_post):
    batch, seq, d = x.shape
    assert ret_w_in.shape[0] == 1 and sb_w_in.shape[0] == 1
    n = batch * seq
    x2d = x.reshape(n, d)

    vec = lambda g: g.reshape(1, d).astype(F32)

    proj = _in_proj(x2d, vec(ret_norm_pre[0]), ret_w_in[0].astype(BF16), seq=seq)
    o = _retention(proj, batch=batch, seq=seq)
    qkw = SB_HEADS * SB_QK_DIM
    q_scale = math.log2(math.e) / math.sqrt(SB_QK_DIM)
    col_scale = jnp.ones((1, sb_w_in.shape[2]), F32).at[:, :qkw].set(q_scale)
    ret_width = RET_HEADS * RET_V_DIM
    sb_width = SB_HEADS * SB_V_DIM
    h1, k_hm, v_hm, q_hm, gate2 = _out_proj(
        o, (proj, ret_width, 2), ret_w_out[0].astype(BF16), vec(ret_norm_post[0]), x2d,
        [(vec(kv_norm), w_kv.astype(BF16), None, [(qkw, SB_QK_DIM), (sb_width, SB_V_DIM)]),
         (vec(sb_norm_pre[0]), sb_w_in[0].astype(BF16), col_scale, [(qkw, SB_QK_DIM), (sb_width, 0)])],
        tm=512, sub_m=512, name="ret_out_kv_q_proj")
    o2 = _sb_attention(q_hm, k_hm, v_hm, batch=batch, seq=seq)
    (out,) = _out_proj(o2, (gate2, sb_width, 0), sb_w_out[0].astype(BF16), vec(sb_norm_post[0]),
                       h1, [], tm=1024, sub_m=256, name="sb_out_proj")
    return out.reshape(batch, seq, d)
```

```python
import functools
import math

import jax
import jax.numpy as jnp
from jax import lax
from jax.experimental import pallas as pl
from jax.experimental.pallas import tpu as pltpu

EPS = 1e-6
ROPE_BASE = 10000.0

RET_HEADS = 4
RET_QK_DIM = 256
RET_V_DIM = 512
RET_CHUNK = 256

SB_HEADS = 8
SB_QK_DIM = 128
SB_V_DIM = 256

VMEM_LIMIT_BYTES = 56 * 1024 * 1024

MASKED_LOG2_WEIGHT = -1e30

F32 = jnp.float32
BF16 = jnp.bfloat16


def _params(semantics):
    return pltpu.CompilerParams(dimension_semantics=semantics,
                                vmem_limit_bytes=VMEM_LIMIT_BYTES)


def _rms_scale(x):
    return x * lax.rsqrt(jnp.mean(x * x, axis=-1, keepdims=True) + EPS)


def _silu(g):
    return g / (1.0 + jnp.exp2(g * (-math.log2(math.e))))


def _in_proj_kernel(x_ref, gain_ref, w_ref, cos_a_ref, sin_a_ref, cos_b_ref, sin_b_ref, dec_ref,
                    o_ref, *, n_rope_groups, pos_blocks, sub_m, sub_n):
    half = RET_QK_DIM // 2
    rope_cols = n_rope_groups * RET_QK_DIM
    blk = pl.program_id(0) % pos_blocks
    cos_a = cos_a_ref[pl.ds(blk, 1), :]
    sin_a = sin_a_ref[pl.ds(blk, 1), :]
    for r0 in range(0, o_ref.shape[0], sub_m):
        rs = slice(r0, r0 + sub_m)
        u = (_rms_scale(x_ref[rs, :]) * gain_ref[...]).astype(BF16)
        cos_b = cos_b_ref[rs, :]
        sin_b = sin_b_ref[rs, :]
        cos = cos_a * cos_b - sin_a * sin_b
        sin = sin_a * cos_b + cos_a * sin_b
        for grp in range(n_rope_groups):
            c0 = grp * RET_QK_DIM
            acc = jnp.dot(u, w_ref[:, c0:c0 + RET_QK_DIM], preferred_element_type=F32)
            dec = dec_ref[grp, rs, :]
            cos_g = cos * dec
            sin_g = sin * dec
            x1 = acc[:, :half]
            x2 = acc[:, half:]
            o_ref[rs, c0:c0 + half] = (x1 * cos_g - x2 * sin_g).astype(o_ref.dtype)
            o_ref[rs, c0 + half:c0 + RET_QK_DIM] = (x1 * sin_g + x2 * cos_g).astype(o_ref.dtype)
        for c0 in range(rope_cols, o_ref.shape[1], sub_n):
            o_ref[rs, c0:c0 + sub_n] = jnp.dot(
                u, w_ref[:, c0:c0 + sub_n], preferred_element_type=F32).astype(o_ref.dtype)


def _in_proj(x2d, gain, w, *, seq, tm=512, sub_m=256):
    n, d = x2d.shape
    n_out = w.shape[1]
    n_rope_groups = 2 * RET_HEADS
    half = RET_QK_DIM // 2
    assert n % tm == 0 and seq % tm == 0 and tm % RET_CHUNK == 0
    assert (n_out - n_rope_groups * RET_QK_DIM) % 512 == 0
    pos_blocks = seq // tm
    inv_freq = 1.0 / (ROPE_BASE ** (jnp.arange(half, dtype=F32) / half))
    ang_a = (jnp.arange(pos_blocks, dtype=F32) * tm)[:, None] * inv_freq[None, :]
    ang_b = jnp.arange(tm, dtype=F32)[:, None] * inv_freq[None, :]
    log_gamma = jnp.log1p(-jnp.exp2(-5.0 - jnp.arange(RET_HEADS, dtype=F32)))
    steps = (jnp.arange(tm, dtype=F32) % RET_CHUNK) + 1.0
    q_dec = jnp.exp(log_gamma[:, None] * steps[None, :])
    k_dec = jnp.exp(-log_gamma[:, None] * steps[None, :]) * (RET_QK_DIM ** -0.5)
    dec = jnp.broadcast_to(jnp.concatenate([q_dec, k_dec], axis=0)[:, :, None],
                           (n_rope_groups, tm, half))
    kern = functools.partial(_in_proj_kernel, n_rope_groups=n_rope_groups, pos_blocks=pos_blocks,
                             sub_m=sub_m, sub_n=512)
    whole = lambda a: pl.BlockSpec(a.shape, lambda i: (0,) * a.ndim, pipeline_mode=pl.Buffered(1))
    tables = [jnp.cos(ang_a), jnp.sin(ang_a), jnp.cos(ang_b), jnp.sin(ang_b), dec]
    return pl.pallas_call(
        kern,
        grid=(n // tm,),
        in_specs=[pl.BlockSpec((tm, d), lambda i: (i, 0)), whole(gain), whole(w)]
        + [whole(t) for t in tables],
        out_specs=pl.BlockSpec((tm, n_out), lambda i: (i, 0)),
        out_shape=jax.ShapeDtypeStruct((n, n_out), BF16),
        compiler_params=_params(("parallel",)),
        name="ret_in_proj",
    )(x2d, gain, w, *tables)


def _retention_kernel(q_ref, k_ref, v_ref, cdec_ref, o_ref, state_ref, *, chunks_per_step):
    t = pl.program_id(1)

    @pl.when(t == 0)
    def _():
        state_ref[...] = jnp.zeros_like(state_ref)

    c = RET_CHUNK
    causal = (lax.broadcasted_iota(jnp.int32, (c, c), 0)
              >= lax.broadcasted_iota(jnp.int32, (c, c), 1))
    for ci in range(chunks_per_step):
        r0 = ci * c
        for h in range(RET_HEADS):
            q = q_ref[r0:r0 + c, h * RET_QK_DIM:(h + 1) * RET_QK_DIM]
            k = k_ref[r0:r0 + c, h * RET_QK_DIM:(h + 1) * RET_QK_DIM]
            v = v_ref[r0:r0 + c, h * RET_V_DIM:(h + 1) * RET_V_DIM]
            state = state_ref[h]
            scores = lax.dot_general(q, k, (((1,), (1,)), ((), ())), preferred_element_type=F32)
            scores = jnp.where(causal, scores, 0.0)
            o = jnp.dot(jnp.concatenate([scores.astype(BF16), q], axis=1),
                        jnp.concatenate([v, state.astype(BF16)], axis=0),
                        preferred_element_type=F32)
            kv = lax.dot_general(k, v, (((0,), (0,)), ((), ())), preferred_element_type=F32)
            state_ref[h] = (state + kv) * cdec_ref[h]
            mu = jnp.mean(o, axis=-1, keepdims=True)
            d = o - mu
            var = jnp.mean(d * d, axis=-1, keepdims=True)
            on = d * lax.rsqrt(var + EPS)
            o_ref[r0:r0 + c, h * RET_V_DIM:(h + 1) * RET_V_DIM] = on.astype(o_ref.dtype)


def _retention(proj, *, batch, seq, rows=1024):
    n = proj.shape[0]
    qw = RET_HEADS * RET_QK_DIM
    vw = RET_HEADS * RET_V_DIM
    assert proj.shape[1] == 2 * qw + 2 * vw and vw == 2 * qw
    assert seq % rows == 0 and rows % RET_CHUNK == 0
    steps = seq // rows
    log_gamma = jnp.log1p(-jnp.exp2(-5.0 - jnp.arange(RET_HEADS, dtype=F32)))
    cdec = jnp.broadcast_to(jnp.exp(log_gamma * RET_CHUNK)[:, None, None],
                            (RET_HEADS, 1, RET_V_DIM))
    kern = functools.partial(_retention_kernel, chunks_per_step=rows // RET_CHUNK)
    row = lambda b, t: b * steps + t
    return pl.pallas_call(
        kern,
        grid=(batch, steps),
        in_specs=[
            pl.BlockSpec((rows, qw), lambda b, t: (row(b, t), 0)),
            pl.BlockSpec((rows, qw), lambda b, t: (row(b, t), 1)),
            pl.BlockSpec((rows, vw), lambda b, t: (row(b, t), 1)),
            pl.BlockSpec(cdec.shape, lambda b, t: (0, 0, 0)),
        ],
        out_specs=pl.BlockSpec((rows, vw), lambda b, t: (row(b, t), 0)),
        out_shape=jax.ShapeDtypeStruct((n, vw), BF16),
        scratch_shapes=[pltpu.VMEM((RET_HEADS, RET_QK_DIM, RET_V_DIM), F32)],
        compiler_params=_params(("parallel", "arbitrary")),
        name="retention",
    )(proj, proj, proj, cdec)


def _out_proj_kernel(o_ref, *rest, n_gate, layout, sub_m, sub_n):
    gate_refs = rest[:n_gate]
    w_ref, gain_ref, res_ref = rest[n_gate:n_gate + 3]
    rest = rest[n_gate + 3:]
    nexts = []
    pos = 0
    for has_scale, _ in layout:
        width = 3 if has_scale else 2
        nexts.append(rest[pos:pos + width])
        pos += width
    h_ref = rest[pos]
    out_refs = list(rest[pos + 1:])
    for r0 in range(0, o_ref.shape[0], sub_m):
        rs = slice(r0, r0 + sub_m)
        gated = []
        c0 = 0
        for g_ref in gate_refs:
            cs = slice(c0, c0 + g_ref.shape[1])
            gated.append((o_ref[rs, cs].astype(F32) * _silu(g_ref[rs, :].astype(F32))).astype(BF16))
            c0 = cs.stop
        assert c0 == o_ref.shape[1]
        y = jnp.dot(jnp.concatenate(gated, axis=1), w_ref[...], preferred_element_type=F32)
        h = res_ref[rs, :] + _rms_scale(y) * gain_ref[...]
        h_ref[rs, :] = h
        if layout:
            hn = _rms_scale(h)
        seg_refs = iter(out_refs)
        for refs, (_, segments) in zip(nexts, layout):
            u = (hn * refs[0][...]).astype(BF16)
            col = 0
            for width, head_width in segments:
                out_ref = next(seg_refs)
                for c0 in range(col, col + width, sub_n):
                    cs = slice(c0, c0 + sub_n)
                    acc = jnp.dot(u, refs[1][:, cs], preferred_element_type=F32)
                    if len(refs) == 3:
                        acc = acc * refs[2][:, cs]
                    if head_width:
                        for t in range(sub_n // head_width):
                            head = (c0 - col) // head_width + t
                            out_ref[head, rs, :] = acc[:, t * head_width:(t + 1) * head_width].astype(
                                out_ref.dtype)
                    else:
                        out_ref[rs, c0 - col:c0 - col + sub_n] = acc.astype(out_ref.dtype)
                col += width


def _out_proj(o, gate, w, gain, res, nexts, *, tm, sub_m, name, sub_n=512):
    n, kdim = o.shape
    d = w.shape[1]
    row_spec = lambda width, col=0: pl.BlockSpec((tm, width), lambda i: (i, col))
    whole = lambda a: pl.BlockSpec(a.shape, lambda i: (0, 0), pipeline_mode=pl.Buffered(1))
    gate_arr, gate_width, gate_block0 = gate
    n_gate = kdim // gate_width
    assert n_gate * gate_width == kdim
    operands = [o] + [gate_arr] * n_gate + [w, gain, res]
    in_specs = ([row_spec(kdim)] + [row_spec(gate_width, gate_block0 + t) for t in range(n_gate)]
                + [whole(w), whole(gain), row_spec(d)])
    layout = []
    out_shape = [jax.ShapeDtypeStruct((n, d), F32)]
    out_specs = [row_spec(d)]
    for gain_i, w_i, scale_i, segments in nexts:
        group = [gain_i, w_i] + ([] if scale_i is None else [scale_i])
        operands += group
        in_specs += [whole(a) for a in group]
        layout.append((scale_i is not None, tuple(segments)))
        assert sum(width for width, _ in segments) == w_i.shape[1]
        for width, head_width in segments:
            assert width % sub_n == 0 and (head_width == 0 or sub_n % head_width == 0)
            if head_width:
                heads = width // head_width
                out_shape.append(jax.ShapeDtypeStruct((heads, n, head_width), BF16))
                out_specs.append(pl.BlockSpec((heads, tm, head_width), lambda i: (0, i, 0)))
            else:
                out_shape.append(jax.ShapeDtypeStruct((n, width), BF16))
                out_specs.append(row_spec(width))
    return pl.pallas_call(
        functools.partial(_out_proj_kernel, n_gate=n_gate, layout=tuple(layout), sub_m=sub_m,
                          sub_n=sub_n),
        grid=(n // tm,),
        in_specs=in_specs,
        out_specs=out_specs,
        out_shape=out_shape,
        compiler_params=_params(("parallel",)),
        name=name,
    )(*operands)


def _sb_kernel(ph_ref, pq_ref, pj_ref, q_ref, k_ref, v_ref, mm_ref, o_ref,
               za0_ref, za1_ref, zb0_ref, zb1_ref, pba0_ref, pba1_ref, pbb0_ref, pbb1_ref,
               zpa0_ref, zpa1_ref, zpb0_ref, zpb1_ref, p0a0_ref, p0a1_ref, p0b0_ref, p0b1_ref,
               wa_ref, wb_ref, acc_ref, carry_ref, *, tq, tk, nq, n_masked, n_pairs):
    z_refs = ((za0_ref, za1_ref), (zb0_ref, zb1_ref))
    pb_refs = ((pba0_ref, pba1_ref), (pbb0_ref, pbb1_ref))
    p0_refs = ((p0a0_ref, p0a1_ref), (p0b0_ref, p0b1_ref))
    zp_refs = ((zpa0_ref, zpa1_ref), (zpb0_ref, zpb1_ref))
    w_refs = (wa_ref, wb_ref)

    def rows(qi):
        return pl.ds(pl.multiple_of(qi * tq, tq), tq)

    def live_rows(masked, e):
        return slice(tk if (masked and e == 0) else 0, tq)

    def slot(p):
        return ph_ref[p] * nq + pq_ref[p]

    def qk(p, par, masked):
        hd, qi, jh = ph_ref[p], pq_ref[p], pj_ref[p]
        q = q_ref[hd, rows(qi), :]
        for e in range(2):
            rs = live_rows(masked, e)
            kb = k_ref[hd, pl.ds(pl.multiple_of((jh - e) * tk, tk), tk), :]
            z_refs[par][e][rs, :] = lax.dot_general(q[rs, :], kb, (((1,), (1,)), ((), ())),
                                                    preferred_element_type=F32)

    def score(p, par, masked):
        for e in range(2):
            rs = live_rows(masked, e)
            z2 = z_refs[par][e][rs, :]
            t = jnp.log2(1.0 + jnp.exp2(-jnp.abs(z2)))
            pos = jnp.maximum(z2, 0.0) + t
            zp = z2 - pos
            if masked:
                causal = (lax.broadcasted_iota(jnp.int32, z2.shape, 1)
                          < lax.broadcasted_iota(jnp.int32, z2.shape, 0))
                pos = jnp.where(causal, pos, 0.0)
                zp = jnp.where(causal, zp, MASKED_LOG2_WEIGHT)
            pb_refs[par][e][rs, :] = pos.astype(BF16)
            p0_refs[par][e][rs, :] = pos[:, 0:1]
            zp_refs[par][e][rs, :] = zp

    def weights(p, par, first):
        a = slot(p)
        carry = None if first else carry_ref[a]
        for e in range(2):
            rs = live_rows(first, e)
            cols = slice((1 - e) * tk, (2 - e) * tk)
            sfx = jnp.dot(pb_refs[par][e][rs, :], mm_ref[...], preferred_element_type=F32)
            x = zp_refs[par][e][rs, :] - sfx
            if carry is not None:
                x = x - carry
            w_refs[par][rs, cols] = jnp.exp2(x).astype(BF16)
            row_sum = sfx[:, 0:1] + p0_refs[par][e][rs, :]
            if rs.start:
                w_refs[par][0:rs.start, cols] = jnp.zeros((rs.start, tk), BF16)
                row_sum = jnp.concatenate([jnp.zeros((rs.start, 1), F32), row_sum], axis=0)
            carry = row_sum if carry is None else carry + row_sum
        carry_ref[a] = carry

    def pv(p, par, first):
        a, jh = slot(p), pj_ref[p]
        vb = v_ref[ph_ref[p], pl.ds(pl.multiple_of((jh - 1) * tk, tk), 2 * tk), :]
        out = jnp.dot(w_refs[par][...], vb, preferred_element_type=F32)
        if first:
            acc_ref[a] = out
        else:
            acc_ref[a] += out

    def body(b, par, masked_of):
        if masked_of(2) is not None:
            weights(b - 2, par, first=masked_of(2))
        if masked_of(3) is not None:
            pv(b - 3, 1 - par, first=masked_of(3))
        if masked_of(0) is not None:
            qk(b, par, masked_of(0))
        if masked_of(1) is not None:
            score(b - 1, 1 - par, masked=masked_of(1))

    def static_flags(b):
        def masked_of(s):
            p = b - s
            return None if not 0 <= p < n_pairs else p < n_masked
        return masked_of

    def peel(lo, hi):
        for b in range(lo, hi):
            body(b, b % 2, static_flags(b))

    def loop(lo, hi, flag, unroll):
        assert (hi - lo) % unroll == 0 and unroll % 2 == 0

        def step(m, c):
            for d in range(unroll):
                body(lo + unroll * m + d, (lo + d) % 2, lambda s: flag)
            return c
        lax.fori_loop(0, (hi - lo) // unroll, step, 0)

    depth = 3
    m_unroll, u_unroll = 6, 8
    m_lo = depth + (n_masked - depth) % m_unroll
    u_lo = n_masked + depth + (n_pairs - n_masked - depth) % u_unroll
    assert m_lo <= n_masked and u_lo <= n_pairs
    peel(0, m_lo)
    loop(m_lo, n_masked, True, m_unroll)
    peel(n_masked, u_lo)
    loop(u_lo, n_pairs, False, u_unroll)
    peel(n_pairs, n_pairs + depth)

    dv = acc_ref.shape[2]
    for a in range(acc_ref.shape[0]):
        hd, qi = divmod(a, nq)
        o_ref[qi * tq:(qi + 1) * tq, hd * dv:(hd + 1) * dv] = acc_ref[a].astype(o_ref.dtype)


def _sb_attention(q_hm, k_hm, v_hm, *, batch, seq, tq=512, tk=256, heads_per_step=2):
    n_heads, n, dv = v_hm.shape
    assert q_hm.shape == k_hm.shape == (n_heads, n, SB_QK_DIM) and n == batch * seq
    assert seq % tq == 0 and tq == 2 * tk and n_heads % heads_per_step == 0
    nq = seq // tq
    hps = heads_per_step
    rr = lax.broadcasted_iota(jnp.int32, (tk, tk), 0)
    cc = lax.broadcasted_iota(jnp.int32, (tk, tk), 1)
    suffix_mat = (rr > cc).astype(BF16)
    pairs = [(hd, qi, 2 * qi + 1) for hd in range(hps) for qi in range(nq)]
    n_masked = len(pairs)
    pairs += [(hd, qi, jh) for hd in range(hps) for qi in range(nq)
              for jh in range(2 * qi - 1, 0, -2)]
    tables = [jnp.asarray(col, jnp.int32) for col in zip(*pairs)]
    kern = functools.partial(_sb_kernel, tq=tq, tk=tk, nq=nq, n_masked=n_masked,
                             n_pairs=len(pairs))
    stage_f32 = pltpu.VMEM((tq, tk), F32)
    stage_bf16 = pltpu.VMEM((tq, tk), BF16)
    stage_col = pltpu.VMEM((tq, 1), F32)
    head_block = lambda width: pl.BlockSpec((hps, seq, width), lambda b, g, *_: (g, b, 0))
    return pl.pallas_call(
        kern,
        grid_spec=pltpu.PrefetchScalarGridSpec(
            num_scalar_prefetch=len(tables),
            grid=(batch, n_heads // hps),
            in_specs=[head_block(SB_QK_DIM), head_block(SB_QK_DIM), head_block(dv),
                      pl.BlockSpec((tk, tk), lambda b, g, *_: (0, 0))],
            out_specs=pl.BlockSpec((seq, hps * dv), lambda b, g, *_: (b, g)),
            scratch_shapes=[stage_f32] * 4 + [stage_bf16] * 4 + [stage_f32] * 4 + [stage_col] * 4
            + [pltpu.VMEM((tq, 2 * tk), BF16)] * 2 + [
                pltpu.VMEM((hps * nq, tq, dv), F32), pltpu.VMEM((hps * nq, tq, 1), F32)],
        ),
        out_shape=jax.ShapeDtypeStruct((n, n_heads * dv), BF16),
        compiler_params=_params(("parallel", "parallel")),
        name="sb_attention",
    )(*tables, q_hm, k_hm, v_hm, suffix_mat)


def kernel(x, ret_norm_pre, ret_w_in, ret_w_out, ret_norm_post, kv_norm, w_kv,
           sb_norm_pre, sb_w_in, sb_w_out, sb_norm_post):
    batch, seq, d = x.shape
    assert ret_w_in.shape[0] == 1 and sb_w_in.shape[0] == 1
    n = batch * seq
    x2d = x.reshape(n, d)

    vec = lambda g: g.reshape(1, d).astype(F32)

    proj = _in_proj(x2d, vec(ret_norm_pre[0]), ret_w_in[0].astype(BF16), seq=seq)
    o = _retention(proj, batch=batch, seq=seq)
    qkw = SB_HEADS * SB_QK_DIM
    q_scale = math.log2(math.e) / math.sqrt(SB_QK_DIM)
    col_scale = jnp.ones((1, sb_w_in.shape[2]), F32).at[:, :qkw].set(q_scale)
    ret_width = RET_HEADS * RET_V_DIM
    sb_width = SB_HEADS * SB_V_DIM
    h1, k_hm, v_hm, q_hm, gate2 = _out_proj(
        o, (proj, ret_width, 2), ret_w_out[0].astype(BF16), vec(ret_norm_post[0]), x2d,
        [(vec(kv_norm), w_kv.astype(BF16), None, [(qkw, SB_QK_DIM), (sb_width, SB_V_DIM)]),
         (vec(sb_norm_pre[0]), sb_w_in[0].astype(BF16), col_scale, [(qkw, SB_QK_DIM), (sb_width, 0)])],
        tm=512, sub_m=512, name="ret_out_kv_q_proj")
    o2 = _sb_attention(q_hm, k_hm, v_hm, batch=batch, seq=seq)
    (out,) = _out_proj(o2, (gate2, sb_width, 0), sb_w_out[0].astype(BF16), vec(sb_norm_post[0]),
                       h1, [], tm=1024, sub_m=256, name="sb_out_proj")
    return out.reshape(batch, seq, d)
```

```python
import functools
import math

import jax
import jax.numpy as jnp
from jax import lax
from jax.experimental import pallas as pl
from jax.experimental.pallas import tpu as pltpu

EPS = 1e-6
ROPE_BASE = 10000.0

RET_HEADS = 4
RET_QK_DIM = 256
RET_V_DIM = 512
RET_CHUNK = 256

SB_HEADS = 8
SB_QK_DIM = 128
SB_V_DIM = 256

VMEM_LIMIT_BYTES = 56 * 1024 * 1024

MASKED_LOG2_WEIGHT = -1e30

F32 = jnp.float32
BF16 = jnp.bfloat16


def _params(semantics):
    return pltpu.CompilerParams(dimension_semantics=semantics,
                                vmem_limit_bytes=VMEM_LIMIT_BYTES)


def _rms_scale(x):
    return x * lax.rsqrt(jnp.mean(x * x, axis=-1, keepdims=True) + EPS)


def _silu(g):
    return g / (1.0 + jnp.exp2(g * (-math.log2(math.e))))


def _in_proj_kernel(x_ref, gain_ref, w_ref, cos_a_ref, sin_a_ref, cos_b_ref, sin_b_ref, dec_ref,
                    o_ref, *, n_rope_groups, pos_blocks, sub_m, sub_n):
    half = RET_QK_DIM // 2
    rope_cols = n_rope_groups * RET_QK_DIM
    blk = pl.program_id(0) % pos_blocks
    cos_a = cos_a_ref[pl.ds(blk, 1), :]
    sin_a = sin_a_ref[pl.ds(blk, 1), :]
    for r0 in range(0, o_ref.shape[0], sub_m):
        rs = slice(r0, r0 + sub_m)
        u = (_rms_scale(x_ref[rs, :]) * gain_ref[...]).astype(BF16)
        cos_b = cos_b_ref[rs, :]
        sin_b = sin_b_ref[rs, :]
        cos = cos_a * cos_b - sin_a * sin_b
        sin = sin_a * cos_b + cos_a * sin_b
        for grp in range(n_rope_groups):
            c0 = grp * RET_QK_DIM
            acc = jnp.dot(u, w_ref[:, c0:c0 + RET_QK_DIM], preferred_element_type=F32)
            dec = dec_ref[grp, rs, :]
            cos_g = cos * dec
            sin_g = sin * dec
            x1 = acc[:, :half]
            x2 = acc[:, half:]
            o_ref[rs, c0:c0 + half] = (x1 * cos_g - x2 * sin_g).astype(o_ref.dtype)
            o_ref[rs, c0 + half:c0 + RET_QK_DIM] = (x1 * sin_g + x2 * cos_g).astype(o_ref.dtype)
        for c0 in range(rope_cols, o_ref.shape[1], sub_n):
            o_ref[rs, c0:c0 + sub_n] = jnp.dot(
                u, w_ref[:, c0:c0 + sub_n], preferred_element_type=F32).astype(o_ref.dtype)


def _in_proj(x2d, gain, w, *, seq, tm=512, sub_m=256):
    n, d = x2d.shape
    n_out = w.shape[1]
    n_rope_groups = 2 * RET_HEADS
    half = RET_QK_DIM // 2
    assert n % tm == 0 and seq % tm == 0 and tm % RET_CHUNK == 0
    assert (n_out - n_rope_groups * RET_QK_DIM) % 512 == 0
    pos_blocks = seq // tm
    inv_freq = 1.0 / (ROPE_BASE ** (jnp.arange(half, dtype=F32) / half))
    ang_a = (jnp.arange(pos_blocks, dtype=F32) * tm)[:, None] * inv_freq[None, :]
    ang_b = jnp.arange(tm, dtype=F32)[:, None] * inv_freq[None, :]
    log_gamma = jnp.log1p(-jnp.exp2(-5.0 - jnp.arange(RET_HEADS, dtype=F32)))
    steps = (jnp.arange(tm, dtype=F32) % RET_CHUNK) + 1.0
    q_dec = jnp.exp(log_gamma[:, None] * steps[None, :])
    k_dec = jnp.exp(-log_gamma[:, None] * steps[None, :]) * (RET_QK_DIM ** -0.5)
    dec = jnp.broadcast_to(jnp.concatenate([q_dec, k_dec], axis=0)[:, :, None],
                           (n_rope_groups, tm, half))
    kern = functools.partial(_in_proj_kernel, n_rope_groups=n_rope_groups, pos_blocks=pos_blocks,
                             sub_m=sub_m, sub_n=512)
    whole = lambda a: pl.BlockSpec(a.shape, lambda i: (0,) * a.ndim, pipeline_mode=pl.Buffered(1))
    tables = [jnp.cos(ang_a), jnp.sin(ang_a), jnp.cos(ang_b), jnp.sin(ang_b), dec]
    return pl.pallas_call(
        kern,
        grid=(n // tm,),
        in_specs=[pl.BlockSpec((tm, d), lambda i: (i, 0)), whole(gain), whole(w)]
        + [whole(t) for t in tables],
        out_specs=pl.BlockSpec((tm, n_out), lambda i: (i, 0)),
        out_shape=jax.ShapeDtypeStruct((n, n_out), BF16),
        compiler_params=_params(("parallel",)),
        name="ret_in_proj",
    )(x2d, gain, w, *tables)


def _retention_kernel(q_ref, k_ref, v_ref, cdec_ref, o_ref, state_ref, *, chunks_per_step):
    t = pl.program_id(1)

    @pl.when(t == 0)
    def _():
        state_ref[...] = jnp.zeros_like(state_ref)

    c = RET_CHUNK
    causal = (lax.broadcasted_iota(jnp.int32, (c, c), 0)
              >= lax.broadcasted_iota(jnp.int32, (c, c), 1))
    for ci in range(chunks_per_step):
        r0 = ci * c
        for h in range(RET_HEADS):
            q = q_ref[r0:r0 + c, h * RET_QK_DIM:(h + 1) * RET_QK_DIM]
            k = k_ref[r0:r0 + c, h * RET_QK_DIM:(h + 1) * RET_QK_DIM]
            v = v_ref[r0:r0 + c, h * RET_V_DIM:(h + 1) * RET_V_DIM]
            state = state_ref[h]
            scores = lax.dot_general(q, k, (((1,), (1,)), ((), ())), preferred_element_type=F32)
            scores = jnp.where(causal, scores, 0.0)
            o = jnp.dot(jnp.concatenate([scores.astype(BF16), q], axis=1),
                        jnp.concatenate([v, state.astype(BF16)], axis=0),
                        preferred_element_type=F32)
            kv = lax.dot_general(k, v, (((0,), (0,)), ((), ())), preferred_element_type=F32)
            state_ref[h] = (state + kv) * cdec_ref[h]
            mu = jnp.mean(o, axis=-1, keepdims=True)
            d = o - mu
            var = jnp.mean(d * d, axis=-1, keepdims=True)
            on = d * lax.rsqrt(var + EPS)
            o_ref[r0:r0 + c, h * RET_V_DIM:(h + 1) * RET_V_DIM] = on.astype(o_ref.dtype)


def _retention(proj, *, batch, seq, rows=1024):
    n = proj.shape[0]
    qw = RET_HEADS * RET_QK_DIM
    vw = RET_HEADS * RET_V_DIM
    assert proj.shape[1] == 2 * qw + 2 * vw and vw == 2 * qw
    assert seq % rows == 0 and rows % RET_CHUNK == 0
    steps = seq // rows
    log_gamma = jnp.log1p(-jnp.exp2(-5.0 - jnp.arange(RET_HEADS, dtype=F32)))
    cdec = jnp.broadcast_to(jnp.exp(log_gamma * RET_CHUNK)[:, None, None],
                            (RET_HEADS, 1, RET_V_DIM))
    kern = functools.partial(_retention_kernel, chunks_per_step=rows // RET_CHUNK)
    row = lambda b, t: b * steps + t
    return pl.pallas_call(
        kern,
        grid=(batch, steps),
        in_specs=[
            pl.BlockSpec((rows, qw), lambda b, t: (row(b, t), 0)),
            pl.BlockSpec((rows, qw), lambda b, t: (row(b, t), 1)),
            pl.BlockSpec((rows, vw), lambda b, t: (row(b, t), 1)),
            pl.BlockSpec(cdec.shape, lambda b, t: (0, 0, 0)),
        ],
        out_specs=pl.BlockSpec((rows, vw), lambda b, t: (row(b, t), 0)),
        out_shape=jax.ShapeDtypeStruct((n, vw), BF16),
        scratch_shapes=[pltpu.VMEM((RET_HEADS, RET_QK_DIM, RET_V_DIM), F32)],
        compiler_params=_params(("parallel", "arbitrary")),
        name="retention",
    )(proj, proj, proj, cdec)


def _out_proj_kernel(o_ref, *rest, n_gate, layout, sub_m, sub_n):
    gate_refs = rest[:n_gate]
    w_ref, gain_ref, res_ref = rest[n_gate:n_gate + 3]
    rest = rest[n_gate + 3:]
    nexts = []
    pos = 0
    for has_scale, _ in layout:
        width = 3 if has_scale else 2
        nexts.append(rest[pos:pos + width])
        pos += width
    h_ref = rest[pos]
    out_refs = list(rest[pos + 1:])
    for r0 in range(0, o_ref.shape[0], sub_m):
        rs = slice(r0, r0 + sub_m)
        gated = []
        c0 = 0
        for g_ref in gate_refs:
            cs = slice(c0, c0 + g_ref.shape[1])
            gated.append((o_ref[rs, cs].astype(F32) * _silu(g_ref[rs, :].astype(F32))).astype(BF16))
            c0 = cs.stop
        assert c0 == o_ref.shape[1]
        y = jnp.dot(jnp.concatenate(gated, axis=1), w_ref[...], preferred_element_type=F32)
        h = res_ref[rs, :] + _rms_scale(y) * gain_ref[...]
        h_ref[rs, :] = h
        if layout:
            hn = _rms_scale(h)
        seg_refs = iter(out_refs)
        for refs, (_, segments) in zip(nexts, layout):
            u = (hn * refs[0][...]).astype(BF16)
            col = 0
            for width, head_width in segments:
                out_ref = next(seg_refs)
                for c0 in range(col, col + width, sub_n):
                    cs = slice(c0, c0 + sub_n)
                    acc = jnp.dot(u, refs[1][:, cs], preferred_element_type=F32)
                    if len(refs) == 3:
                        acc = acc * refs[2][:, cs]
                    if head_width:
                        for t in range(sub_n // head_width):
                            head = (c0 - col) // head_width + t
                            out_ref[head, rs, :] = acc[:, t * head_width:(t + 1) * head_width].astype(
                                out_ref.dtype)
                    else:
                        out_ref[rs, c0 - col:c0 - col + sub_n] = acc.astype(out_ref.dtype)
                col += width


def _out_proj(o, gate, w, gain, res, nexts, *, tm, sub_m, name, sub_n=512):
    n, kdim = o.shape
    d = w.shape[1]
    row_spec = lambda width, col=0: pl.BlockSpec((tm, width), lambda i: (i, col))
    whole = lambda a: pl.BlockSpec(a.shape, lambda i: (0, 0), pipeline_mode=pl.Buffered(1))
    gate_arr, gate_width, gate_block0 = gate
    n_gate = kdim // gate_width
    assert n_gate * gate_width == kdim
    operands = [o] + [gate_arr] * n_gate + [w, gain, res]
    in_specs = ([row_spec(kdim)] + [row_spec(gate_width, gate_block0 + t) for t in range(n_gate)]
                + [whole(w), whole(gain), row_spec(d)])
    layout = []
    out_shape = [jax.ShapeDtypeStruct((n, d), F32)]
    out_specs = [row_spec(d)]
    for gain_i, w_i, scale_i, segments in nexts:
        group = [gain_i, w_i] + ([] if scale_i is None else [scale_i])
        operands += group
        in_specs += [whole(a) for a in group]
        layout.append((scale_i is not None, tuple(segments)))
        assert sum(width for width, _ in segments) == w_i.shape[1]
        for width, head_width in segments:
            assert width % sub_n == 0 and (head_width == 0 or sub_n % head_width == 0)
            if head_width:
                heads = width // head_width
                out_shape.append(jax.ShapeDtypeStruct((heads, n, head_width), BF16))
                out_specs.append(pl.BlockSpec((heads, tm, head_width), lambda i: (0, i, 0)))
            else:
                out_shape.append(jax.ShapeDtypeStruct((n, width), BF16))
                out_specs.append(row_spec(width))
    return pl.pallas_call(
        functools.partial(_out_proj_kernel, n_gate=n_gate, layout=tuple(layout), sub_m=sub_m,
                          sub_n=sub_n),
        grid=(n // tm,),
        in_specs=in_specs,
        out_specs=out_specs,
        out_shape=out_shape,
        compiler_params=_params(("parallel",)),
        name=name,
    )(*operands)


def _sb_kernel(ph_ref, pq_ref, pj_ref, q_ref, k_ref, v_ref, mm_ref, o_ref,
               za0_ref, za1_ref, zb0_ref, zb1_ref, pba0_ref, pba1_ref, pbb0_ref, pbb1_ref,
               zpa0_ref, zpa1_ref, zpb0_ref, zpb1_ref, p0a0_ref, p0a1_ref, p0b0_ref, p0b1_ref,
               wa_ref, wb_ref, acc_ref, carry_ref, *, tq, tk, nq, n_masked, n_pairs):
    z_refs = ((za0_ref, za1_ref), (zb0_ref, zb1_ref))
    pb_refs = ((pba0_ref, pba1_ref), (pbb0_ref, pbb1_ref))
    p0_refs = ((p0a0_ref, p0a1_ref), (p0b0_ref, p0b1_ref))
    zp_refs = ((zpa0_ref, zpa1_ref), (zpb0_ref, zpb1_ref))
    w_refs = (wa_ref, wb_ref)

    def rows(qi):
        return pl.ds(pl.multiple_of(qi * tq, tq), tq)

    def live_rows(masked, e):
        return slice(tk if (masked and e == 0) else 0, tq)

    def slot(p):
        return ph_ref[p] * nq + pq_ref[p]

    def qk(p, par, masked):
        hd, qi, jh = ph_ref[p], pq_ref[p], pj_ref[p]
        q = q_ref[hd, rows(qi), :]
        for e in range(2):
            rs = live_rows(masked, e)
            kb = k_ref[hd, pl.ds(pl.multiple_of((jh - e) * tk, tk), tk), :]
            z_refs[par][e][rs, :] = lax.dot_general(q[rs, :], kb, (((1,), (1,)), ((), ())),
                                                    preferred_element_type=F32)

    def score(p, par, masked):
        half = tk // 2
        for e in range(2):
            rs = live_rows(masked, e)
            for c0 in (0, half):
                cs = slice(c0, c0 + half)
                z2 = z_refs[par][e][rs, cs]
                t = jnp.log2(1.0 + jnp.exp2(-jnp.abs(z2)))
                pos = jnp.maximum(z2, 0.0) + t
                zp = z2 - pos
                if masked:
                    causal = (lax.broadcasted_iota(jnp.int32, z2.shape, 1) + c0
                              < lax.broadcasted_iota(jnp.int32, z2.shape, 0))
                    pos = jnp.where(causal, pos, 0.0)
                    zp = jnp.where(causal, zp, MASKED_LOG2_WEIGHT)
                pb_refs[par][e][rs, cs] = pos.astype(BF16)
                if c0 == 0:
                    p0_refs[par][e][rs, :] = pos[:, 0:1]
                zp_refs[par][e][rs, cs] = zp

    def weights(p, par, first):
        a = slot(p)
        carry = None if first else carry_ref[a]
        for e in range(2):
            rs = live_rows(first, e)
            cols = slice((1 - e) * tk, (2 - e) * tk)
            sfx = jnp.dot(pb_refs[par][e][rs, :], mm_ref[...], preferred_element_type=F32)
            x = zp_refs[par][e][rs, :] - sfx
            if carry is not None:
                x = x - carry
            w_refs[par][rs, cols] = jnp.exp2(x).astype(BF16)
            row_sum = sfx[:, 0:1] + p0_refs[par][e][rs, :]
            if rs.start:
                w_refs[par][0:rs.start, cols] = jnp.zeros((rs.start, tk), BF16)
                row_sum = jnp.concatenate([jnp.zeros((rs.start, 1), F32), row_sum], axis=0)
            carry = row_sum if carry is None else carry + row_sum
        carry_ref[a] = carry

    def pv(p, par, first):
        a, jh = slot(p), pj_ref[p]
        vb = v_ref[ph_ref[p], pl.ds(pl.multiple_of((jh - 1) * tk, tk), 2 * tk), :]
        out = jnp.dot(w_refs[par][...], vb, preferred_element_type=F32)
        if first:
            acc_ref[a] = out
        else:
            acc_ref[a] += out

    def body(b, par, masked_of):
        if masked_of(2) is not None:
            weights(b - 2, par, first=masked_of(2))
        if masked_of(3) is not None:
            pv(b - 3, 1 - par, first=masked_of(3))
        if masked_of(0) is not None:
            qk(b, par, masked_of(0))
        if masked_of(1) is not None:
            score(b - 1, 1 - par, masked=masked_of(1))

    def static_flags(b):
        def masked_of(s):
            p = b - s
            return None if not 0 <= p < n_pairs else p < n_masked
        return masked_of

    def peel(lo, hi):
        for b in range(lo, hi):
            body(b, b % 2, static_flags(b))

    def loop(lo, hi, flag, unroll):
        assert (hi - lo) % unroll == 0 and unroll % 2 == 0

        def step(m, c):
            for d in range(unroll):
                body(lo + unroll * m + d, (lo + d) % 2, lambda s: flag)
            return c
        lax.fori_loop(0, (hi - lo) // unroll, step, 0)

    depth = 3
    m_unroll, u_unroll = 4, 8
    m_lo = depth + (n_masked - depth) % m_unroll
    u_lo = n_masked + depth + (n_pairs - n_masked - depth) % u_unroll
    assert m_lo <= n_masked and u_lo <= n_pairs
    peel(0, m_lo)
    loop(m_lo, n_masked, True, m_unroll)
    peel(n_masked, u_lo)
    loop(u_lo, n_pairs, False, u_unroll)
    peel(n_pairs, n_pairs + depth)

    dv = acc_ref.shape[2]
    for a in range(acc_ref.shape[0]):
        hd, qi = divmod(a, nq)
        o_ref[qi * tq:(qi + 1) * tq, hd * dv:(hd + 1) * dv] = acc_ref[a].astype(o_ref.dtype)


def _sb_attention(q_hm, k_hm, v_hm, *, batch, seq, tq=512, tk=256, heads_per_step=2):
    n_heads, n, dv = v_hm.shape
    assert q_hm.shape == k_hm.shape == (n_heads, n, SB_QK_DIM) and n == batch * seq
    assert seq % tq == 0 and tq == 2 * tk and n_heads % heads_per_step == 0
    nq = seq // tq
    hps = heads_per_step
    rr = lax.broadcasted_iota(jnp.int32, (tk, tk), 0)
    cc = lax.broadcasted_iota(jnp.int32, (tk, tk), 1)
    suffix_mat = (rr > cc).astype(BF16)
    pairs = [(hd, qi, 2 * qi + 1) for hd in range(hps) for qi in range(nq)]
    n_masked = len(pairs)
    pairs += [(hd, qi, jh) for hd in range(hps) for qi in range(nq)
              for jh in range(2 * qi - 1, 0, -2)]
    tables = [jnp.asarray(col, jnp.int32) for col in zip(*pairs)]
    kern = functools.partial(_sb_kernel, tq=tq, tk=tk, nq=nq, n_masked=n_masked,
                             n_pairs=len(pairs))
    stage_f32 = pltpu.VMEM((tq, tk), F32)
    stage_bf16 = pltpu.VMEM((tq, tk), BF16)
    stage_col = pltpu.VMEM((tq, 1), F32)
    head_block = lambda width: pl.BlockSpec((hps, seq, width), lambda b, g, *_: (g, b, 0))
    return pl.pallas_call(
        kern,
        grid_spec=pltpu.PrefetchScalarGridSpec(
            num_scalar_prefetch=len(tables),
            grid=(batch, n_heads // hps),
            in_specs=[head_block(SB_QK_DIM), head_block(SB_QK_DIM), head_block(dv),
                      pl.BlockSpec((tk, tk), lambda b, g, *_: (0, 0))],
            out_specs=pl.BlockSpec((seq, hps * dv), lambda b, g, *_: (b, g)),
            scratch_shapes=[stage_f32] * 4 + [stage_bf16] * 4 + [stage_f32] * 4 + [stage_col] * 4
            + [pltpu.VMEM((tq, 2 * tk), BF16)] * 2 + [
                pltpu.VMEM((hps * nq, tq, dv), F32), pltpu.VMEM((hps * nq, tq, 1), F32)],
        ),
        out_shape=jax.ShapeDtypeStruct((n, n_heads * dv), BF16),
        compiler_params=_params(("parallel", "parallel")),
        name="sb_attention",
    )(*tables, q_hm, k_hm, v_hm, suffix_mat)


def kernel(x, ret_norm_pre, ret_w_in, ret_w_out, ret_norm_post, kv_norm, w_kv,
           sb_norm_pre, sb_w_in, sb_w_out, sb_norm_post):
    batch, seq, d = x.shape
    assert ret_w_in.shape[0] == 1 and sb_w_in.shape[0] == 1
    n = batch * seq
    x2d = x.reshape(n, d)

    vec = lambda g: g.reshape(1, d).astype(F32)

    proj = _in_proj(x2d, vec(ret_norm_pre[0]), ret_w_in[0].astype(BF16), seq=seq)
    o = _retention(proj, batch=batch, seq=seq)
    qkw = SB_HEADS * SB_QK_DIM
    q_scale = math.log2(math.e) / math.sqrt(SB_QK_DIM)
    col_scale = jnp.ones((1, sb_w_in.shape[2]), F32).at[:, :qkw].set(q_scale)
    ret_width = RET_HEADS * RET_V_DIM
    sb_width = SB_HEADS * SB_V_DIM
    h1, k_hm, v_hm, q_hm, gate2 = _out_proj(
        o, (proj, ret_width, 2), ret_w_out[0].astype(BF16), vec(ret_norm_post[0]), x2d,
        [(vec(kv_norm), w_kv.astype(BF16), None, [(qkw, SB_QK_DIM), (sb_width, SB_V_DIM)]),
         (vec(sb_norm_pre[0]), sb_w_in[0].astype(BF16), col_scale, [(qkw, SB_QK_DIM), (sb_width, 0)])],
        tm=512, sub_m=512, name="ret_out_kv_q_proj")
    o2 = _sb_attention(q_hm, k_hm, v_hm, batch=batch, seq=seq)
    (out,) = _out_proj(o2, (gate2, sb_width, 0), sb_w_out[0].astype(BF16), vec(sb_norm_post[0]),
                       h1, [], tm=1024, sub_m=256, name="sb_out_proj")
    return out.reshape(batch, seq, d)
```
